```python
import math
import jax, jax.numpy as jnp
from jax import lax
import numpy as np

D_MODEL = 1024
BATCH = 8
SEQ = 2048
DEPTH = 2
DEC_BATCH = 4
DEC_SEQ = 4096
PAST_LEN = 128

N_HEADS_A = 8
N_KV_HEADS_A = 2
GROUP_A = N_HEADS_A // N_KV_HEADS_A
HEAD_DIM_A = 64
WINDOW = 128
BLOCK = 128
N_HEADS_B = 8
QK_NOPE_DIM = 64
QK_ROPE_DIM = 32
V_DIM_B = 64
Q_LORA_RANK = 384
KV_LORA_RANK = 256
ROPE_THETA = 10000.0
N_BUCKETS = 32
MAX_DISTANCE = 128
D_FF = 2816
FFN_RES_WEIGHT = 0.5
EPS = 1e-6

A_Q_COLS = N_HEADS_A * HEAD_DIM_A
A_KV_COLS = N_KV_HEADS_A * HEAD_DIM_A
SPLIT_SIZES = (A_Q_COLS, A_KV_COLS, A_KV_COLS, Q_LORA_RANK, KV_LORA_RANK, QK_ROPE_DIM, D_MODEL, D_MODEL)
IN_COLS = A_Q_COLS + 2 * A_KV_COLS + Q_LORA_RANK + KV_LORA_RANK + QK_ROPE_DIM + 2 * D_MODEL

kernel_name = 'hybrid_gated_swa_mla_encoder'


def rmsnorm(x, g):
    x32 = x.astype(jnp.float32)
    y = x32 * lax.rsqrt(jnp.mean(x32 * x32, axis=-1, keepdims=True) + EPS) * g.astype(jnp.float32)
    return y.astype(x.dtype)


def t5_bucket(rel):
    half = N_BUCKETS // 2
    max_exact = half // 2
    ret = jnp.where(rel > 0, half, 0)
    n = jnp.abs(rel)
    nf = jnp.maximum(n, 1).astype(jnp.float32)
    large = max_exact + (jnp.log(nf / max_exact) / math.log(MAX_DISTANCE / max_exact) * (half - max_exact)).astype(jnp.int32)
    large = jnp.minimum(large, half - 1)
    return ret + jnp.where(n < max_exact, n, large)


def rope_tables(S):
    inv_freq = ROPE_THETA ** (-jnp.arange(0, QK_ROPE_DIM, 2, dtype=jnp.float32) / QK_ROPE_DIM)
    ang = jnp.arange(S, dtype=jnp.float32)[:, None] * inv_freq[None, :]
    return jnp.cos(ang), jnp.sin(ang)


def apply_rope(x, cos, sin):
    half = x.shape[-1] // 2
    x1, x2 = x[..., :half], x[..., half:]
    cos = cos.astype(x.dtype)
    sin = sin.astype(x.dtype)
    return jnp.concatenate([x1 * cos - x2 * sin, x1 * sin + x2 * cos], axis=-1)


def window_gqa(q, k, v, rel_bias, sink):
    B, S, _ = q.shape
    nb = S // BLOCK
    qb = q.reshape(B, nb, BLOCK, N_KV_HEADS_A, GROUP_A, HEAD_DIM_A)
    pad = ((0, 0), (BLOCK, BLOCK), (0, 0))
    kp = jnp.pad(k, pad).reshape(B, nb + 2, BLOCK, N_KV_HEADS_A, HEAD_DIM_A)
    vp = jnp.pad(v, pad).reshape(B, nb + 2, BLOCK, N_KV_HEADS_A, HEAD_DIM_A)
    kw = jnp.concatenate([kp[:, :-2], kp[:, 1:-1], kp[:, 2:]], axis=2)
    vw = jnp.concatenate([vp[:, :-2], vp[:, 1:-1], vp[:, 2:]], axis=2)
    s = jnp.einsum('bnqkgd,bnskd->bnkgqs', qb, kw).astype(jnp.float32) * (HEAD_DIM_A ** -0.5)
    qi = jnp.arange(BLOCK, dtype=jnp.int32)[:, None]
    si = jnp.arange(3 * BLOCK, dtype=jnp.int32)[None, :]
    rel = si - BLOCK - qi
    bias = rel_bias[t5_bucket(rel)].astype(jnp.float32)
    bias = jnp.transpose(bias, (2, 0, 1)).reshape(N_KV_HEADS_A, GROUP_A, BLOCK, 3 * BLOCK)
    s = s + bias[None, None]
    kpos = (jnp.arange(nb, dtype=jnp.int32)[:, None] - 1) * BLOCK + jnp.arange(3 * BLOCK, dtype=jnp.int32)[None, :]
    valid = (kpos >= 0) & (kpos < S)
    mask = valid[:, None, :] & (jnp.abs(rel) <= WINDOW)[None]
    s = jnp.where(mask[None, :, None, None], s, -jnp.inf)
    sk = sink.astype(jnp.float32).reshape(N_KV_HEADS_A, GROUP_A)[None, None, :, :, None]
    lse = jnp.logaddexp(jax.nn.logsumexp(s, axis=-1), sk)
    p = jnp.exp(s - lse[..., None]).astype(v.dtype)
    o = jnp.einsum('bnkgqs,bnskd->bnqkgd', p, vw)
    return o.reshape(B, S, A_Q_COLS)


def mla(cq, ckv, kr, q_norm_g, kv_norm_g, w_uq, w_ukv):
    B, S, _ = cq.shape
    q = (rmsnorm(cq, q_norm_g) @ w_uq).reshape(B, S, N_HEADS_B, QK_NOPE_DIM + QK_ROPE_DIM)
    kv = (rmsnorm(ckv, kv_norm_g) @ w_ukv).reshape(B, S, N_HEADS_B, QK_NOPE_DIM + V_DIM_B)
    q_nope, q_rope = q[..., :QK_NOPE_DIM], q[..., QK_NOPE_DIM:]
    k_nope, v = kv[..., :QK_NOPE_DIM], kv[..., QK_NOPE_DIM:]
    cos, sin = rope_tables(S)
    q_rope = apply_rope(q_rope, cos[:, None, :], sin[:, None, :])
    k_rope = apply_rope(kr, cos, sin)
    scale = (QK_NOPE_DIM + QK_ROPE_DIM) ** -0.5
    nb = S // BLOCK
    qn_b = q_nope.reshape(B, nb, BLOCK, N_HEADS_B, QK_NOPE_DIM).transpose(1, 0, 2, 3, 4)
    qr_b = q_rope.reshape(B, nb, BLOCK, N_HEADS_B, QK_ROPE_DIM).transpose(1, 0, 2, 3, 4)

    def attend(blk):
        qn, qr = blk
        s = jnp.einsum('bqhd,bshd->bhqs', qn, k_nope) + jnp.einsum('bqhr,bsr->bhqs', qr, k_rope)
        p = jax.nn.softmax(s.astype(jnp.float32) * scale, axis=-1).astype(v.dtype)
        return jnp.einsum('bhqs,bshd->bqhd', p, v)

    o = lax.map(attend, (qn_b, qr_b))
    return o.transpose(1, 0, 2, 3, 4).reshape(B, S, N_HEADS_B * V_DIM_B)


def half_step_ffn(x, pre_g, post_g, w_gate, w_up, w_down):
    h = rmsnorm(x, pre_g)
    y = (jax.nn.silu(h @ w_gate) * (h @ w_up)) @ w_down
    return x + FFN_RES_WEIGHT * rmsnorm(y, post_g)


def split_cols(proj):
    idx = []
    acc = 0
    for sz in SPLIT_SIZES[:-1]:
        acc += sz
        idx.append(acc)
    return jnp.split(proj, idx, axis=-1)


def token_mixer(x, rel_bias, pre_g, post_g, w_in, sink, q_norm_g, kv_norm_g, w_uq, w_ukv, w_a_out, w_b_out, w_o):
    h = rmsnorm(x, pre_g)
    qa, ka, va, cq, ckv, kr, ga, gb = split_cols(h @ w_in)
    o_a = window_gqa(qa, ka, va, rel_bias, sink) @ w_a_out
    o_b = mla(cq, ckv, kr, q_norm_g, kv_norm_g, w_uq, w_ukv) @ w_b_out
    merged = jax.nn.sigmoid(ga) * o_a + jax.nn.sigmoid(gb) * o_b
    return x + rmsnorm(merged @ w_o, post_g)


def trunk(x, rel_bias,
          ffn1_pre_g, ffn1_post_g, ffn1_w_gate, ffn1_w_up, ffn1_w_down,
          mix_pre_g, mix_post_g, w_in, sink, q_norm_g, kv_norm_g, w_uq, w_ukv, w_a_out, w_b_out, w_o,
          ffn2_pre_g, ffn2_post_g, ffn2_w_gate, ffn2_w_up, ffn2_w_down):
    for l in range(DEPTH):
        x = half_step_ffn(x, ffn1_pre_g[l], ffn1_post_g[l], ffn1_w_gate[l], ffn1_w_up[l], ffn1_w_down[l])
        x = token_mixer(x, rel_bias, mix_pre_g[l], mix_post_g[l], w_in[l], sink[l], q_norm_g[l], kv_norm_g[l],
                        w_uq[l], w_ukv[l], w_a_out[l], w_b_out[l], w_o[l])
        x = half_step_ffn(x, ffn2_pre_g[l], ffn2_post_g[l], ffn2_w_gate[l], ffn2_w_up[l], ffn2_w_down[l])
    return x


def setup_inputs(seed: int = 0) -> dict:
    key = jax.random.key(seed)
    ks = jax.random.split(key, 32)
    f32 = jnp.float32

    def w(k, shape, fan_in):
        return jax.random.normal(k, shape, f32) * (fan_in ** -0.5)

    def gain(k, shape):
        return 1.0 + 0.05 * jax.random.normal(k, shape, f32)

    D = D_MODEL
    return {
        'x_prompt': jax.random.normal(ks[0], (BATCH, SEQ, D), f32),
        'x_sample': jax.random.normal(ks[1], (DEC_BATCH, DEC_SEQ, D), f32),
        'rel_bias': 0.5 * jax.random.normal(ks[2], (N_BUCKETS, N_HEADS_A), f32),
        'ffn1_pre_g': gain(ks[3], (DEPTH, D)),
        'ffn1_post_g': gain(ks[4], (DEPTH, D)),
        'ffn1_w_gate': w(ks[5], (DEPTH, D, D_FF), D),
        'ffn1_w_up': w(ks[6], (DEPTH, D, D_FF), D),
        'ffn1_w_down': w(ks[7], (DEPTH, D_FF, D), D_FF),
        'mix_pre_g': gain(ks[8], (DEPTH, D)),
        'mix_post_g': gain(ks[9], (DEPTH, D)),
        'w_in': w(ks[10], (DEPTH, D, IN_COLS), D),
        'sink': jax.random.normal(ks[11], (DEPTH, N_HEADS_A), f32),
        'q_norm_g': gain(ks[12], (DEPTH, Q_LORA_RANK)),
        'kv_norm_g': gain(ks[13], (DEPTH, KV_LORA_RANK)),
        'w_uq': w(ks[14], (DEPTH, Q_LORA_RANK, N_HEADS_B * (QK_NOPE_DIM + QK_ROPE_DIM)), Q_LORA_RANK),
        'w_ukv': w(ks[15], (DEPTH, KV_LORA_RANK, N_HEADS_B * (QK_NOPE_DIM + V_DIM_B)), KV_LORA_RANK),
        'w_a_out': w(ks[16], (DEPTH, A_Q_COLS, D), A_Q_COLS),
        'w_b_out': w(ks[17], (DEPTH, N_HEADS_B * V_DIM_B, D), N_HEADS_B * V_DIM_B),
        'w_o': w(ks[18], (DEPTH, D, D), D),
        'ffn2_pre_g': gain(ks[19], (DEPTH, D)),
        'ffn2_post_g': gain(ks[20], (DEPTH, D)),
        'ffn2_w_gate': w(ks[21], (DEPTH, D, D_FF), D),
        'ffn2_w_up': w(ks[22], (DEPTH, D, D_FF), D),
        'ffn2_w_down': w(ks[23], (DEPTH, D_FF, D), D_FF),
    }


def reference(x_prompt, x_sample, rel_bias,
              ffn1_pre_g, ffn1_post_g, ffn1_w_gate, ffn1_w_up, ffn1_w_down,
              mix_pre_g, mix_post_g, w_in, sink, q_norm_g, kv_norm_g, w_uq, w_ukv, w_a_out, w_b_out, w_o,
              ffn2_pre_g, ffn2_post_g, ffn2_w_gate, ffn2_w_up, ffn2_w_down):
    params = (rel_bias,
              ffn1_pre_g, ffn1_post_g, ffn1_w_gate, ffn1_w_up, ffn1_w_down,
              mix_pre_g, mix_post_g, w_in, sink, q_norm_g, kv_norm_g, w_uq, w_ukv, w_a_out, w_b_out, w_o,
              ffn2_pre_g, ffn2_post_g, ffn2_w_gate, ffn2_w_up, ffn2_w_down)
    y_prompt = trunk(x_prompt, *params)
    y_sample = trunk(x_sample, *params)
    return (y_prompt, y_sample)
```

```python
import functools
import math

import jax
import jax.numpy as jnp
from jax import lax
from jax.experimental import pallas as pl
from jax.experimental.pallas import tpu as pltpu

F32 = jnp.float32
BF16 = jnp.bfloat16

D_MODEL = 1024
DEPTH = 2
N_HEADS_A = 8
N_KV_HEADS_A = 2
GROUP_A = N_HEADS_A // N_KV_HEADS_A
HEAD_DIM_A = 64
WINDOW = 128
BLOCK = 128
N_HEADS_B = 8
QK_NOPE_DIM = 64
QK_ROPE_DIM = 32
V_DIM_B = 64
Q_LORA_RANK = 384
KV_LORA_RANK = 256
ROPE_THETA = 10000.0
N_BUCKETS = 32
MAX_DISTANCE = 128
D_FF = 2816
FFN_RES_WEIGHT = 0.5
EPS = 1e-6

A_Q_COLS = N_HEADS_A * HEAD_DIM_A
A_KV_COLS = N_KV_HEADS_A * HEAD_DIM_A

LANES = 128
SLOT = LANES
VMEM_LIMIT_BYTES = 56 * 1024 * 1024

W1_QA = (0, 512)
W1_KA = (512, 768)
W1_VA = (768, 1280)
W1_CQ = (1280, 1664)
W1_CKV = (1664, 1920)
W1_KR = (1920, 2048)
W1_KRR = (2048, 2176)
W1_COLS = 2176

FFN_CHUNKS = ((0, 1024), (1024, 1024), (2048, 768))
FFN_TM = 256
PROJ_TM = 256
MERGE_TM = 256
MLA_TQ = 256
MLA_KC = 512


def _rms(x, g):
    return x * lax.rsqrt(jnp.mean(x * x, axis=-1, keepdims=True) + EPS) * g


def _const_spec(shape):
    nd = len(shape)
    return pl.BlockSpec(shape, lambda *_: (0,) * nd, pipeline_mode=pl.Buffered(1))


def _params(n_axes):
    return pltpu.CompilerParams(
        dimension_semantics=("parallel",) * n_axes,
        vmem_limit_bytes=VMEM_LIMIT_BYTES,
    )


def _ffn_kernel(x_ref, pre_ref, post_ref, wg_ref, wu_ref, wd_ref, o_ref, acc_ref):
    x = x_ref[...]
    hb = _rms(x, pre_ref[...]).astype(BF16)
    for idx, (c0, cw) in enumerate(FFN_CHUNKS):
        g = jnp.dot(hb, wg_ref[:, c0:c0 + cw], preferred_element_type=F32)
        u = jnp.dot(hb, wu_ref[:, c0:c0 + cw], preferred_element_type=F32)
        a = (jax.nn.silu(g) * u).astype(BF16)
        part = jnp.dot(a, wd_ref[c0:c0 + cw, :], preferred_element_type=F32)
        if idx == 0:
            acc_ref[...] = part
        else:
            acc_ref[...] += part
    o_ref[...] = x + FFN_RES_WEIGHT * _rms(acc_ref[...], post_ref[...])


def _ffn(x, pre_g, post_g, wg, wu, wd):
    t = x.shape[0]
    tm = FFN_TM
    row = pl.BlockSpec((tm, D_MODEL), lambda i: (i, 0))
    return pl.pallas_call(
        _ffn_kernel,
        grid=(t // tm,),
        in_specs=[row, _const_spec((1, D_MODEL)), _const_spec((1, D_MODEL)),
                  _const_spec((D_MODEL, D_FF)), _const_spec((D_MODEL, D_FF)),
                  _const_spec((D_FF, D_MODEL))],
        out_specs=row,
        out_shape=jax.ShapeDtypeStruct((t, D_MODEL), F32),
        scratch_shapes=[pltpu.VMEM((tm, D_MODEL), F32)],
        compiler_params=_params(1),
        name="ffn",
    )(x, pre_g, post_g, wg, wu, wd)


def _proj_kernel(x_ref, g_ref, w1_ref, qg_ref, kvg_ref, wq_ref, wkv_ref, cos_ref, sin_ref,
                 qa_ref, ka_ref, va_ref, q_ref, k_ref, v_ref):
    hb = _rms(x_ref[...], g_ref[...]).astype(BF16)
    proj = jnp.dot(hb, w1_ref[...], preferred_element_type=F32)
    qa_ref[...] = proj[:, W1_QA[0]:W1_QA[1]].astype(BF16)
    ka_ref[...] = proj[:, W1_KA[0]:W1_KA[1]].astype(BF16)
    va_ref[...] = proj[:, W1_VA[0]:W1_VA[1]].astype(BF16)
    cqn = _rms(proj[:, W1_CQ[0]:W1_CQ[1]], qg_ref[...]).astype(BF16)
    ckvn = _rms(proj[:, W1_CKV[0]:W1_CKV[1]], kvg_ref[...]).astype(BF16)
    cos = cos_ref[...]
    sin = sin_ref[...]
    k_rope = proj[:, W1_KR[0]:W1_KR[1]] * cos + proj[:, W1_KRR[0]:W1_KRR[1]] * sin
    qf = jnp.dot(cqn, wq_ref[...], preferred_element_type=F32)
    kvf = jnp.dot(ckvn, wkv_ref[...], preferred_element_type=F32)
    rot0 = N_HEADS_B * SLOT
    for h in range(N_HEADS_B):
        lo, hi = h * SLOT, (h + 1) * SLOT
        q_ref[:, lo:hi] = (qf[:, lo:hi] * cos + qf[:, rot0 + lo:rot0 + hi] * sin).astype(BF16)
        k_ref[:, lo:hi] = (kvf[:, lo:hi] + k_rope).astype(BF16)
    v_ref[...] = kvf[:, rot0:].astype(BF16)


def _proj(x, g, w1, qg, kvg, wq, wkv, cos_tab, sin_tab, seq):
    t = x.shape[0]
    tm = PROJ_TM
    nseq = seq // tm

    def row(cols):
        return pl.BlockSpec((tm, cols), lambda i: (i, 0))

    tab = pl.BlockSpec((tm, SLOT), lambda i: (i % nseq, 0))
    hs = N_HEADS_B * SLOT
    out_cols = (A_Q_COLS, W1_KA[1] - W1_KA[0], W1_VA[1] - W1_VA[0], hs, hs, hs)
    return pl.pallas_call(
        _proj_kernel,
        grid=(t // tm,),
        in_specs=[row(D_MODEL), _const_spec((1, D_MODEL)), _const_spec((D_MODEL, W1_COLS)),
                  _const_spec((1, Q_LORA_RANK)), _const_spec((1, KV_LORA_RANK)),
                  _const_spec((Q_LORA_RANK, 2 * hs)), _const_spec((KV_LORA_RANK, 2 * hs)),
                  tab, tab],
        out_specs=[row(c) for c in out_cols],
        out_shape=[jax.ShapeDtypeStruct((t, c), BF16) for c in out_cols],
        compiler_params=_params(1),
        name="proj",
    )(x, g, w1, qg, kvg, wq, wkv, cos_tab, sin_tab)


def _bias_kernel(rb_ref, bucket_ref, o_ref):
    h = pl.program_id(1)
    bucket = bucket_ref[0]
    acc = jnp.full(bucket.shape, -jnp.inf, F32)
    for b in range(N_BUCKETS):
        acc = jnp.where(bucket == b, rb_ref[b, h], acc)
    o_ref[0, 0] = acc


def _bias_table(rel_bias, bucket):
    kw = 3 * BLOCK
    return pl.pallas_call(
        _bias_kernel,
        grid=(3, N_HEADS_A),
        in_specs=[pl.BlockSpec(memory_space=pltpu.SMEM),
                  pl.BlockSpec((1, BLOCK, kw), lambda e, h: (e, 0, 0))],
        out_specs=pl.BlockSpec((1, 1, BLOCK, kw), lambda e, h: (e, h, 0, 0)),
        out_shape=jax.ShapeDtypeStruct((3, N_HEADS_A, BLOCK, kw), F32),
        compiler_params=_params(2),
        name="t5_bias",
    )(rel_bias, bucket)


def _t5_bucket(rel):
    half = N_BUCKETS // 2
    max_exact = half // 2
    ret = jnp.where(rel > 0, half, 0)
    n = jnp.abs(rel)
    nf = jnp.maximum(n, 1).astype(F32)
    large = max_exact + (jnp.log(nf / max_exact) / math.log(MAX_DISTANCE / max_exact)
                         * (half - max_exact)).astype(jnp.int32)
    large = jnp.minimum(large, half - 1)
    return ret + jnp.where(n < max_exact, n, large)


def _band_buckets():
    qi = jnp.arange(BLOCK, dtype=jnp.int32)[:, None]
    si = jnp.arange(3 * BLOCK, dtype=jnp.int32)[None, :]
    out = []
    for shift in (0, -BLOCK, -2 * BLOCK):
        rel = si + shift - qi
        out.append(jnp.where(jnp.abs(rel) <= WINDOW, _t5_bucket(rel), -1))
    return jnp.stack(out).astype(jnp.int32)


def _win_kernel(sink_ref, q_ref, k_ref, v_ref, bias_ref, o_ref, *, seq):
    n = pl.program_id(1)
    nb = pl.num_programs(1)
    kw = 3 * BLOCK
    start = pl.multiple_of(jnp.clip((n - 1) * BLOCK, 0, seq - kw), BLOCK)
    edge = jnp.where(n == 0, 0, jnp.where(n == nb - 1, 2, 1))
    k_win = k_ref[pl.ds(start, kw), :]
    v_win = v_ref[pl.ds(start, kw), :]
    lane = lax.broadcasted_iota(jnp.int32, (BLOCK, SLOT), 1)
    scale = HEAD_DIM_A ** -0.5
    for pair in range(N_HEADS_A // 2):
        kv_head = (2 * pair) // GROUP_A
        q_pair = q_ref[:, pair * SLOT:(pair + 1) * SLOT]
        k_dup = k_win[:, kv_head * SLOT:(kv_head + 1) * SLOT]
        out = None
        for par in range(2):
            h = 2 * pair + par
            keep = (lane < HEAD_DIM_A) if par == 0 else (lane >= HEAD_DIM_A)
            q_h = jnp.where(keep, q_pair, jnp.zeros_like(q_pair))
            s = lax.dot_general(q_h, k_dup, (((1,), (1,)), ((), ())),
                                preferred_element_type=F32) * scale
            s = s + bias_ref[edge, h]
            sink = sink_ref[h]
            m = jnp.maximum(jnp.max(s, axis=-1, keepdims=True), sink)
            p = jnp.exp(s - m)
            denom = jnp.sum(p, axis=-1, keepdims=True) + jnp.exp(sink - m)
            v_h = v_win[:, (2 * kv_head + par) * SLOT:(2 * kv_head + par + 1) * SLOT]
            o_h = jnp.dot(p.astype(BF16), v_h, preferred_element_type=F32) / denom
            out = o_h if out is None else out + o_h
        o_ref[:, pair * SLOT:(pair + 1) * SLOT] = out.astype(BF16)


def _window_attention(sink, qa, ka, va, bias, batch, seq):
    nb = seq // BLOCK
    kc = ka.shape[-1]
    vc = va.shape[-1]
    return pl.pallas_call(
        functools.partial(_win_kernel, seq=seq),
        grid=(batch, nb),
        in_specs=[pl.BlockSpec(memory_space=pltpu.SMEM),
                  pl.BlockSpec((None, BLOCK, A_Q_COLS), lambda b, n: (b, n, 0)),
                  pl.BlockSpec((None, seq, kc), lambda b, n: (b, 0, 0)),
                  pl.BlockSpec((None, seq, vc), lambda b, n: (b, 0, 0)),
                  _const_spec(bias.shape)],
        out_specs=pl.BlockSpec((None, BLOCK, A_Q_COLS), lambda b, n: (b, n, 0)),
        out_shape=jax.ShapeDtypeStruct((batch, seq, A_Q_COLS), BF16),
        compiler_params=_params(2),
        name="window_attn",
    )(sink, qa.reshape(batch, seq, -1), ka.reshape(batch, seq, kc), va.reshape(batch, seq, vc), bias)


def _mla_kernel(q_ref, k_ref, v_ref, o_ref, s_ref, *, seq):
    scale = (QK_NOPE_DIM + QK_ROPE_DIM) ** -0.5
    n_chunks = seq // MLA_KC
    out = None
    for hh in range(2):
        lo, hi = hh * SLOT, (hh + 1) * SLOT
        q_h = q_ref[:, lo:hi]
        m = jnp.full((q_h.shape[0], 1), -jnp.inf, F32)
        for c in range(n_chunks):
            rows = slice(c * MLA_KC, (c + 1) * MLA_KC)
            s = lax.dot_general(q_h, k_ref[rows, lo:hi], (((1,), (1,)), ((), ())),
                                preferred_element_type=F32)
            s_ref[:, rows] = s
            m = jnp.maximum(m, jnp.max(s, axis=-1, keepdims=True))
        denom = jnp.zeros_like(m)
        acc = jnp.zeros((q_h.shape[0], SLOT), F32)
        for c in range(n_chunks):
            rows = slice(c * MLA_KC, (c + 1) * MLA_KC)
            p = jnp.exp((s_ref[:, rows] - m) * scale)
            denom = denom + jnp.sum(p, axis=-1, keepdims=True)
            acc = acc + jnp.dot(p.astype(BF16), v_ref[rows, lo:hi], preferred_element_type=F32)
        o_h = acc / denom
        out = o_h if out is None else out + o_h
    o_ref[...] = out.astype(BF16)


def _mla_attention(q, k, v, batch, seq):
    tq = MLA_TQ
    pairs = N_HEADS_B // 2
    pw = 2 * SLOT
    hs = N_HEADS_B * SLOT
    return pl.pallas_call(
        functools.partial(_mla_kernel, seq=seq),
        grid=(batch, pairs, seq // tq),
        in_specs=[pl.BlockSpec((None, tq, pw), lambda b, p, i: (b, i, p)),
                  pl.BlockSpec((None, seq, pw), lambda b, p, i: (b, 0, p)),
                  pl.BlockSpec((None, seq, pw), lambda b, p, i: (b, 0, p))],
        out_specs=pl.BlockSpec((None, tq, SLOT), lambda b, p, i: (b, i, p)),
        out_shape=jax.ShapeDtypeStruct((batch, seq, N_HEADS_B * V_DIM_B), BF16),
        scratch_shapes=[pltpu.VMEM((tq, seq), F32)],
        compiler_params=_params(3),
        name="mla_attn",
    )(q.reshape(batch, seq, hs), k.reshape(batch, seq, hs), v.reshape(batch, seq, hs))


def _merge_kernel(x_ref, oa_ref, ob_ref, pre_ref, post_ref, wga_ref, wgb_ref, wa_ref, wb_ref,
                  wo_ref, o_ref):
    x = x_ref[...]
    hb = _rms(x, pre_ref[...]).astype(BF16)
    ga = jax.nn.sigmoid(jnp.dot(hb, wga_ref[...], preferred_element_type=F32))
    merged = ga * jnp.dot(oa_ref[...], wa_ref[...], preferred_element_type=F32)
    gb = jax.nn.sigmoid(jnp.dot(hb, wgb_ref[...], preferred_element_type=F32))
    merged = merged + gb * jnp.dot(ob_ref[...], wb_ref[...], preferred_element_type=F32)
    y = jnp.dot(merged.astype(BF16), wo_ref[...], preferred_element_type=F32)
    o_ref[...] = x + _rms(y, post_ref[...])


def _merge(x, oa, ob, pre_g, post_g, wga, wgb, wa, wb, wo):
    t = x.shape[0]
    tm = MERGE_TM

    def row(cols):
        return pl.BlockSpec((tm, cols), lambda i: (i, 0))

    bcols = N_HEADS_B * V_DIM_B
    return pl.pallas_call(
        _merge_kernel,
        grid=(t // tm,),
        in_specs=[row(D_MODEL), row(A_Q_COLS), row(bcols),
                  _const_spec((1, D_MODEL)), _const_spec((1, D_MODEL)),
                  _const_spec((D_MODEL, D_MODEL)), _const_spec((D_MODEL, D_MODEL)),
                  _const_spec((A_Q_COLS, D_MODEL)), _const_spec((bcols, D_MODEL)),
                  _const_spec((D_MODEL, D_MODEL))],
        out_specs=row(D_MODEL),
        out_shape=jax.ShapeDtypeStruct((t, D_MODEL), F32),
        compiler_params=_params(1),
        name="merge",
    )(x, oa.reshape(t, -1), ob.reshape(t, -1), pre_g, post_g, wga, wgb, wa, wb, wo)


def _rotate_half_cols(w):
    half = w.shape[-1] // 2
    return jnp.concatenate([-w[..., half:], w[..., :half]], axis=-1)


def _layer_weights(l, w_in, w_uq, w_ukv):
    wi = w_in[l]
    o = 0
    pieces = {}
    for name, sz in (("qa", A_Q_COLS), ("ka", A_KV_COLS), ("va", A_KV_COLS), ("cq", Q_LORA_RANK),
                     ("ckv", KV_LORA_RANK), ("kr", QK_ROPE_DIM), ("ga", D_MODEL), ("gb", D_MODEL)):
        pieces[name] = wi[:, o:o + sz]
        o += sz
    z64 = jnp.zeros((D_MODEL, 64), F32)
    z32 = jnp.zeros((D_MODEL, 32), F32)
    ka = pieces["ka"]
    va = pieces["va"]
    k0, k1 = ka[:, :HEAD_DIM_A], ka[:, HEAD_DIM_A:]
    v0, v1 = va[:, :HEAD_DIM_A], va[:, HEAD_DIM_A:]
    kr = pieces["kr"]
    w1 = jnp.concatenate([
        pieces["qa"],
        k0, k0, k1, k1,
        v0, z64, z64, v0, v1, z64, z64, v1,
        pieces["cq"], pieces["ckv"],
        z64, kr, z32,
        z64, _rotate_half_cols(kr), z32,
    ], axis=1).astype(BF16)

    uq = w_uq[l].reshape(Q_LORA_RANK, N_HEADS_B, QK_NOPE_DIM + QK_ROPE_DIM)
    q_nope, q_rope = uq[..., :QK_NOPE_DIM], uq[..., QK_NOPE_DIM:]
    zq = jnp.zeros((Q_LORA_RANK, N_HEADS_B, SLOT - QK_NOPE_DIM - QK_ROPE_DIM), F32)
    zq64 = jnp.zeros((Q_LORA_RANK, N_HEADS_B, QK_NOPE_DIM), F32)
    q_main = jnp.concatenate([q_nope, q_rope, zq], axis=-1).reshape(Q_LORA_RANK, -1)
    q_rot = jnp.concatenate([zq64, _rotate_half_cols(q_rope), zq], axis=-1).reshape(Q_LORA_RANK, -1)
    wq = jnp.concatenate([q_main, q_rot], axis=1).astype(BF16)

    ukv = w_ukv[l].reshape(KV_LORA_RANK, N_HEADS_B, QK_NOPE_DIM + V_DIM_B)
    k_nope, v_b = ukv[..., :QK_NOPE_DIM], ukv[..., QK_NOPE_DIM:]
    zk = jnp.zeros((KV_LORA_RANK, N_HEADS_B, SLOT - QK_NOPE_DIM), F32)
    k_slots = jnp.concatenate([k_nope, zk], axis=-1).reshape(KV_LORA_RANK, -1)
    zv = jnp.zeros((KV_LORA_RANK, N_HEADS_B // 2, V_DIM_B), F32)
    v_even = jnp.concatenate([v_b[:, 0::2], zv], axis=-1)
    v_odd = jnp.concatenate([zv, v_b[:, 1::2]], axis=-1)
    v_slots = jnp.stack([v_even, v_odd], axis=2).reshape(KV_LORA_RANK, -1)
    wkv = jnp.concatenate([k_slots, v_slots], axis=1).astype(BF16)
    return w1, wq, wkv, pieces["ga"].astype(BF16), pieces["gb"].astype(BF16)


def _rope_slot_tables(seq):
    inv_freq = ROPE_THETA ** (-jnp.arange(0, QK_ROPE_DIM, 2, dtype=F32) / QK_ROPE_DIM)
    ang = jnp.arange(seq, dtype=F32)[:, None] * inv_freq[None, :]
    cos, sin = jnp.cos(ang), jnp.sin(ang)
    pad = jnp.zeros((seq, SLOT - QK_NOPE_DIM - QK_ROPE_DIM), F32)
    cos_tab = jnp.concatenate([jnp.ones((seq, QK_NOPE_DIM), F32), cos, cos, pad], axis=1)
    sin_tab = jnp.concatenate([jnp.zeros((seq, QK_NOPE_DIM), F32), sin, sin, pad], axis=1)
    return cos_tab, sin_tab


def _trunk(x3, bias, layers):
    batch, seq, _ = x3.shape
    x = x3.reshape(batch * seq, D_MODEL)
    cos_tab, sin_tab = _rope_slot_tables(seq)
    for lw in layers:
        x = _ffn(x, *lw["ffn1"])
        qa, ka, va, q, k, v = _proj(x, lw["mix_pre_g"], lw["w1"], lw["q_norm_g"], lw["kv_norm_g"],
                                    lw["wq"], lw["wkv"], cos_tab, sin_tab, seq)
        oa = _window_attention(lw["sink"], qa, ka, va, bias, batch, seq)
        ob = _mla_attention(q, k, v, batch, seq)
        x = _merge(x, oa, ob, lw["mix_pre_g"], lw["mix_post_g"], lw["wga"], lw["wgb"],
                   lw["w_a_out"], lw["w_b_out"], lw["w_o"])
        x = _ffn(x, *lw["ffn2"])
    return x.reshape(batch, seq, D_MODEL)


def kernel(x_prompt, x_sample, rel_bias, ffn1_pre_g, ffn1_post_g, ffn1_w_gate, ffn1_w_up, ffn1_w_down, mix_pre_g, mix_post_g, w_in, sink, q_norm_g, kv_norm_g, w_uq, w_ukv, w_a_out, w_b_out, w_o, ffn2_pre_g, ffn2_post_g, ffn2_w_gate, ffn2_w_up, ffn2_w_down):
    def vec(g, l):
        return g[l].reshape(1, -1)

    layers = []
    for l in range(DEPTH):
        w1, wq, wkv, wga, wgb = _layer_weights(l, w_in, w_uq, w_ukv)
        layers.append(dict(
            ffn1=(vec(ffn1_pre_g, l), vec(ffn1_post_g, l), ffn1_w_gate[l].astype(BF16),
                  ffn1_w_up[l].astype(BF16), ffn1_w_down[l].astype(BF16)),
            ffn2=(vec(ffn2_pre_g, l), vec(ffn2_post_g, l), ffn2_w_gate[l].astype(BF16),
                  ffn2_w_up[l].astype(BF16), ffn2_w_down[l].astype(BF16)),
            mix_pre_g=vec(mix_pre_g, l), mix_post_g=vec(mix_post_g, l),
            q_norm_g=vec(q_norm_g, l), kv_norm_g=vec(kv_norm_g, l),
            w1=w1, wq=wq, wkv=wkv, wga=wga, wgb=wgb, sink=sink[l],
            w_a_out=w_a_out[l].astype(BF16), w_b_out=w_b_out[l].astype(BF16),
            w_o=w_o[l].astype(BF16),
        ))
    bias = _bias_table(rel_bias, _band_buckets())
    return _trunk(x_prompt, bias, layers), _trunk(x_sample, bias, layers)
```

```python
import functools
import math

import jax
import jax.numpy as jnp
from jax import lax
from jax.experimental import pallas as pl
from jax.experimental.pallas import tpu as pltpu

F32 = jnp.float32
BF16 = jnp.bfloat16

D_MODEL = 1024
DEPTH = 2
N_HEADS_A = 8
N_KV_HEADS_A = 2
GROUP_A = N_HEADS_A // N_KV_HEADS_A
HEAD_DIM_A = 64
WINDOW = 128
BLOCK = 128
N_HEADS_B = 8
QK_NOPE_DIM = 64
QK_ROPE_DIM = 32
V_DIM_B = 64
Q_LORA_RANK = 384
KV_LORA_RANK = 256
ROPE_THETA = 10000.0
N_BUCKETS = 32
MAX_DISTANCE = 128
D_FF = 2816
FFN_RES_WEIGHT = 0.5
EPS = 1e-6

A_Q_COLS = N_HEADS_A * HEAD_DIM_A
A_KV_COLS = N_KV_HEADS_A * HEAD_DIM_A

LANES = 128
SLOT = LANES
VMEM_LIMIT_BYTES = 56 * 1024 * 1024

W1_QA = (0, 512)
W1_KA = (512, 768)
W1_VA = (768, 1280)
W1_CQ = (1280, 1664)
W1_CKV = (1664, 1920)
W1_KR = (1920, 2048)
W1_KRR = (2048, 2176)
W1_COLS = 2176

FFN_CHUNKS = ((0, 1024), (1024, 1024), (2048, 768))
FFN_TM = 256
PROJ_TM = 256
MERGE_TM = 256
MLA_TQ = 256
MLA_KC = 512


def _rms(x, g):
    return x * lax.rsqrt(jnp.mean(x * x, axis=-1, keepdims=True) + EPS) * g


def _const_spec(shape):
    nd = len(shape)
    return pl.BlockSpec(shape, lambda *_: (0,) * nd, pipeline_mode=pl.Buffered(1))


def _params(n_axes, flags=None):
    return pltpu.CompilerParams(
        dimension_semantics=("parallel",) * n_axes,
        vmem_limit_bytes=VMEM_LIMIT_BYTES,
        flags=flags,
    )


def _ffn_kernel(x_ref, pre_ref, post_ref, wg_ref, wu_ref, wd_ref, o_ref, acc_ref):
    x = x_ref[...]
    hb = _rms(x, pre_ref[...]).astype(BF16)
    for idx, (c0, cw) in enumerate(FFN_CHUNKS):
        g = jnp.dot(hb, wg_ref[:, c0:c0 + cw], preferred_element_type=F32)
        u = jnp.dot(hb, wu_ref[:, c0:c0 + cw], preferred_element_type=F32)
        a = (jax.nn.silu(g) * u).astype(BF16)
        part = jnp.dot(a, wd_ref[c0:c0 + cw, :], preferred_element_type=F32)
        if idx == 0:
            acc_ref[...] = part
        else:
            acc_ref[...] += part
    o_ref[...] = x + FFN_RES_WEIGHT * _rms(acc_ref[...], post_ref[...])


def _ffn(x, pre_g, post_g, wg, wu, wd):
    t = x.shape[0]
    tm = FFN_TM
    row = pl.BlockSpec((tm, D_MODEL), lambda i: (i, 0))
    return pl.pallas_call(
        _ffn_kernel,
        grid=(t // tm,),
        in_specs=[row, _const_spec((1, D_MODEL)), _const_spec((1, D_MODEL)),
                  _const_spec((D_MODEL, D_FF)), _const_spec((D_MODEL, D_FF)),
                  _const_spec((D_FF, D_MODEL))],
        out_specs=row,
        out_shape=jax.ShapeDtypeStruct((t, D_MODEL), F32),
        scratch_shapes=[pltpu.VMEM((tm, D_MODEL), F32)],
        compiler_params=_params(1),
        name="ffn",
    )(x, pre_g, post_g, wg, wu, wd)


def _proj_kernel(x_ref, g_ref, w1_ref, qg_ref, kvg_ref, wq_ref, wkv_ref, cos_ref, sin_ref,
                 qa_ref, ka_ref, va_ref, q_ref, k_ref, v_ref):
    hb = _rms(x_ref[...], g_ref[...]).astype(BF16)
    proj = jnp.dot(hb, w1_ref[...], preferred_element_type=F32)
    qa_ref[...] = proj[:, W1_QA[0]:W1_QA[1]].astype(BF16)
    ka_ref[...] = proj[:, W1_KA[0]:W1_KA[1]].astype(BF16)
    va_ref[...] = proj[:, W1_VA[0]:W1_VA[1]].astype(BF16)
    cqn = _rms(proj[:, W1_CQ[0]:W1_CQ[1]], qg_ref[...]).astype(BF16)
    ckvn = _rms(proj[:, W1_CKV[0]:W1_CKV[1]], kvg_ref[...]).astype(BF16)
    cos = cos_ref[...]
    sin = sin_ref[...]
    k_rope = proj[:, W1_KR[0]:W1_KR[1]] * cos + proj[:, W1_KRR[0]:W1_KRR[1]] * sin
    qf = jnp.dot(cqn, wq_ref[...], preferred_element_type=F32)
    kvf = jnp.dot(ckvn, wkv_ref[...], preferred_element_type=F32)
    rot0 = N_HEADS_B * SLOT
    for h in range(N_HEADS_B):
        lo, hi = h * SLOT, (h + 1) * SLOT
        q_ref[:, lo:hi] = (qf[:, lo:hi] * cos + qf[:, rot0 + lo:rot0 + hi] * sin).astype(BF16)
        k_ref[:, lo:hi] = (kvf[:, lo:hi] + k_rope).astype(BF16)
    v = kvf[:, rot0:]
    pair_lane = lax.broadcasted_iota(jnp.int32, v.shape, 1) % (2 * SLOT)
    v_ref[...] = jnp.where((pair_lane == V_DIM_B) | (pair_lane == SLOT), 1.0, v).astype(BF16)


def _proj(x, g, w1, qg, kvg, wq, wkv, cos_tab, sin_tab, seq):
    t = x.shape[0]
    tm = PROJ_TM
    nseq = seq // tm

    def row(cols):
        return pl.BlockSpec((tm, cols), lambda i: (i, 0))

    tab = pl.BlockSpec((tm, SLOT), lambda i: (i % nseq, 0))
    hs = N_HEADS_B * SLOT
    out_cols = (A_Q_COLS, W1_KA[1] - W1_KA[0], W1_VA[1] - W1_VA[0], hs, hs, hs)
    return pl.pallas_call(
        _proj_kernel,
        grid=(t // tm,),
        in_specs=[row(D_MODEL), _const_spec((1, D_MODEL)), _const_spec((D_MODEL, W1_COLS)),
                  _const_spec((1, Q_LORA_RANK)), _const_spec((1, KV_LORA_RANK)),
                  _const_spec((Q_LORA_RANK, 2 * hs)), _const_spec((KV_LORA_RANK, 2 * hs)),
                  tab, tab],
        out_specs=[row(c) for c in out_cols],
        out_shape=[jax.ShapeDtypeStruct((t, c), BF16) for c in out_cols],
        compiler_params=_params(1),
        name="proj",
    )(x, g, w1, qg, kvg, wq, wkv, cos_tab, sin_tab)


def _bias_kernel(rb_ref, bucket_ref, o_ref):
    h = pl.program_id(1)
    bucket = bucket_ref[0]
    acc = jnp.full(bucket.shape, -jnp.inf, F32)
    for b in range(N_BUCKETS):
        acc = jnp.where(bucket == b, rb_ref[b, h], acc)
    o_ref[0, 0] = acc


def _bias_table(rel_bias, bucket):
    kw = 3 * BLOCK
    return pl.pallas_call(
        _bias_kernel,
        grid=(3, N_HEADS_A),
        in_specs=[pl.BlockSpec(memory_space=pltpu.SMEM),
                  pl.BlockSpec((1, BLOCK, kw), lambda e, h: (e, 0, 0))],
        out_specs=pl.BlockSpec((1, 1, BLOCK, kw), lambda e, h: (e, h, 0, 0)),
        out_shape=jax.ShapeDtypeStruct((3, N_HEADS_A, BLOCK, kw), F32),
        compiler_params=_params(2),
        name="t5_bias",
    )(rel_bias, bucket)


def _t5_bucket(rel):
    half = N_BUCKETS // 2
    max_exact = half // 2
    ret = jnp.where(rel > 0, half, 0)
    n = jnp.abs(rel)
    nf = jnp.maximum(n, 1).astype(F32)
    large = max_exact + (jnp.log(nf / max_exact) / math.log(MAX_DISTANCE / max_exact)
                         * (half - max_exact)).astype(jnp.int32)
    large = jnp.minimum(large, half - 1)
    return ret + jnp.where(n < max_exact, n, large)


def _band_buckets():
    qi = jnp.arange(BLOCK, dtype=jnp.int32)[:, None]
    si = jnp.arange(3 * BLOCK, dtype=jnp.int32)[None, :]
    out = []
    for shift in (0, -BLOCK, -2 * BLOCK):
        rel = si + shift - qi
        out.append(jnp.where(jnp.abs(rel) <= WINDOW, _t5_bucket(rel), -1))
    return jnp.stack(out).astype(jnp.int32)


def _win_kernel(sink_ref, q_ref, k_ref, v_ref, bias_ref, o_ref, *, seq):
    n = pl.program_id(1)
    nb = pl.num_programs(1)
    kw = 3 * BLOCK
    start = pl.multiple_of(jnp.clip((n - 1) * BLOCK, 0, seq - kw), BLOCK)
    edge = jnp.where(n == 0, 0, jnp.where(n == nb - 1, 2, 1))
    k_win = k_ref[pl.ds(start, kw), :]
    v_win = v_ref[pl.ds(start, kw), :]
    lane = lax.broadcasted_iota(jnp.int32, (BLOCK, SLOT), 1)
    scale = HEAD_DIM_A ** -0.5
    for pair in range(N_HEADS_A // 2):
        kv_head = (2 * pair) // GROUP_A
        q_pair = q_ref[:, pair * SLOT:(pair + 1) * SLOT]
        k_dup = k_win[:, kv_head * SLOT:(kv_head + 1) * SLOT]
        out = None
        for par in range(2):
            h = 2 * pair + par
            keep = (lane < HEAD_DIM_A) if par == 0 else (lane >= HEAD_DIM_A)
            q_h = jnp.where(keep, q_pair, jnp.zeros_like(q_pair))
            s = lax.dot_general(q_h, k_dup, (((1,), (1,)), ((), ())),
                                preferred_element_type=F32) * scale
            s = s + bias_ref[edge, h]
            sink = sink_ref[h]
            m = jnp.maximum(jnp.max(s, axis=-1, keepdims=True), sink)
            p = jnp.exp(s - m)
            denom = jnp.sum(p, axis=-1, keepdims=True) + jnp.exp(sink - m)
            v_h = v_win[:, (2 * kv_head + par) * SLOT:(2 * kv_head + par + 1) * SLOT]
            o_h = jnp.dot(p.astype(BF16), v_h, preferred_element_type=F32) / denom
            out = o_h if out is None else out + o_h
        o_ref[:, pair * SLOT:(pair + 1) * SLOT] = out.astype(BF16)


def _window_attention(sink, qa, ka, va, bias, batch, seq):
    nb = seq // BLOCK
    kc = ka.shape[-1]
    vc = va.shape[-1]
    return pl.pallas_call(
        functools.partial(_win_kernel, seq=seq),
        grid=(batch, nb),
        in_specs=[pl.BlockSpec(memory_space=pltpu.SMEM),
                  pl.BlockSpec((None, BLOCK, A_Q_COLS), lambda b, n: (b, n, 0)),
                  pl.BlockSpec((None, seq, kc), lambda b, n: (b, 0, 0)),
                  pl.BlockSpec((None, seq, vc), lambda b, n: (b, 0, 0)),
                  _const_spec(bias.shape)],
        out_specs=pl.BlockSpec((None, BLOCK, A_Q_COLS), lambda b, n: (b, n, 0)),
        out_shape=jax.ShapeDtypeStruct((batch, seq, A_Q_COLS), BF16),
        compiler_params=_params(2),
        name="window_attn",
    )(sink, qa.reshape(batch, seq, -1), ka.reshape(batch, seq, kc), va.reshape(batch, seq, vc), bias)


def _mla_kernel(q_ref, k_ref, v_ref, o_ref, s0_ref, s1_ref, *, seq):
    c2 = (QK_NOPE_DIM + QK_ROPE_DIM) ** -0.5 * math.log2(math.e)
    n_chunks = seq // MLA_KC
    tq = q_ref.shape[0]
    lane = lax.broadcasted_iota(jnp.int32, (tq, SLOT), 1)
    s_bufs = (s0_ref, s1_ref)
    row0 = pl.multiple_of(jnp.minimum(pl.program_id(1), 0), 8)
    tile_rows = pl.ds(row0, tq)

    def scores(h):
        lo, hi = h * SLOT, (h + 1) * SLOT
        q_h = q_ref[:, lo:hi]
        m = jnp.full((tq, 1), -jnp.inf, F32)
        for c in range(n_chunks):
            rows = slice(c * MLA_KC, (c + 1) * MLA_KC)
            s = lax.dot_general(q_h, k_ref[rows, lo:hi], (((1,), (1,)), ((), ())),
                                preferred_element_type=F32)
            s_bufs[h % 2][tile_rows, rows] = s
            m = jnp.maximum(m, jnp.max(s, axis=-1, keepdims=True))
        return m

    def weighted(h, m):
        lo, hi = h * SLOT, (h + 1) * SLOT
        acc = jnp.zeros((tq, SLOT), F32)
        for c in range(n_chunks):
            rows = slice(c * MLA_KC, (c + 1) * MLA_KC)
            p = jnp.exp2((s_bufs[h % 2][tile_rows, rows] - m) * c2)
            acc = acc + jnp.dot(p.astype(BF16), v_ref[rows, lo:hi], preferred_element_type=F32)
        return acc

    m = scores(0)
    even = None
    for h in range(N_HEADS_B):
        m_next = scores(h + 1) if h + 1 < N_HEADS_B else None
        acc = weighted(h, m)
        if h % 2 == 0:
            even = acc
        else:
            denom_even = even[:, V_DIM_B:V_DIM_B + 1]
            denom_odd = acc[:, 0:1]
            pair = jnp.where(lane < V_DIM_B, even / denom_even, acc / denom_odd)
            o_ref[:, (h // 2) * SLOT:(h // 2 + 1) * SLOT] = pair.astype(BF16)
        m = m_next


def _mla_attention(q, k, v, batch, seq):
    tq = MLA_TQ
    hs = N_HEADS_B * SLOT
    oc = N_HEADS_B * V_DIM_B
    return pl.pallas_call(
        functools.partial(_mla_kernel, seq=seq),
        grid=(batch, seq // tq),
        in_specs=[pl.BlockSpec((None, tq, hs), lambda b, i: (b, i, 0)),
                  pl.BlockSpec((None, seq, hs), lambda b, i: (b, 0, 0)),
                  pl.BlockSpec((None, seq, hs), lambda b, i: (b, 0, 0))],
        out_specs=pl.BlockSpec((None, tq, oc), lambda b, i: (b, i, 0)),
        out_shape=jax.ShapeDtypeStruct((batch, seq, oc), BF16),
        scratch_shapes=[pltpu.VMEM((tq, seq), F32), pltpu.VMEM((tq, seq), F32)],
        compiler_params=_params(2),
        name="mla_attn",
    )(q.reshape(batch, seq, hs), k.reshape(batch, seq, hs), v.reshape(batch, seq, hs))


def _merge_kernel(x_ref, oa_ref, ob_ref, pre_ref, post_ref, wga_ref, wgb_ref, wa_ref, wb_ref,
                  wo_ref, o_ref):
    x = x_ref[...]
    hb = _rms(x, pre_ref[...]).astype(BF16)
    ga = jax.nn.sigmoid(jnp.dot(hb, wga_ref[...], preferred_element_type=F32))
    merged = ga * jnp.dot(oa_ref[...], wa_ref[...], preferred_element_type=F32)
    gb = jax.nn.sigmoid(jnp.dot(hb, wgb_ref[...], preferred_element_type=F32))
    merged = merged + gb * jnp.dot(ob_ref[...], wb_ref[...], preferred_element_type=F32)
    y = jnp.dot(merged.astype(BF16), wo_ref[...], preferred_element_type=F32)
    o_ref[...] = x + _rms(y, post_ref[...])


def _merge(x, oa, ob, pre_g, post_g, wga, wgb, wa, wb, wo):
    t = x.shape[0]
    tm = MERGE_TM

    def row(cols):
        return pl.BlockSpec((tm, cols), lambda i: (i, 0))

    bcols = N_HEADS_B * V_DIM_B
    return pl.pallas_call(
        _merge_kernel,
        grid=(t // tm,),
        in_specs=[row(D_MODEL), row(A_Q_COLS), row(bcols),
                  _const_spec((1, D_MODEL)), _const_spec((1, D_MODEL)),
                  _const_spec((D_MODEL, D_MODEL)), _const_spec((D_MODEL, D_MODEL)),
                  _const_spec((A_Q_COLS, D_MODEL)), _const_spec((bcols, D_MODEL)),
                  _const_spec((D_MODEL, D_MODEL))],
        out_specs=row(D_MODEL),
        out_shape=jax.ShapeDtypeStruct((t, D_MODEL), F32),
        compiler_params=_params(1),
        name="merge",
    )(x, oa.reshape(t, -1), ob.reshape(t, -1), pre_g, post_g, wga, wgb, wa, wb, wo)


def _rotate_half_cols(w):
    half = w.shape[-1] // 2
    return jnp.concatenate([-w[..., half:], w[..., :half]], axis=-1)


def _layer_weights(l, w_in, w_uq, w_ukv):
    wi = w_in[l]
    o = 0
    pieces = {}
    for name, sz in (("qa", A_Q_COLS), ("ka", A_KV_COLS), ("va", A_KV_COLS), ("cq", Q_LORA_RANK),
                     ("ckv", KV_LORA_RANK), ("kr", QK_ROPE_DIM), ("ga", D_MODEL), ("gb", D_MODEL)):
        pieces[name] = wi[:, o:o + sz]
        o += sz
    z64 = jnp.zeros((D_MODEL, 64), F32)
    z32 = jnp.zeros((D_MODEL, 32), F32)
    ka = pieces["ka"]
    va = pieces["va"]
    k0, k1 = ka[:, :HEAD_DIM_A], ka[:, HEAD_DIM_A:]
    v0, v1 = va[:, :HEAD_DIM_A], va[:, HEAD_DIM_A:]
    kr = pieces["kr"]
    w1 = jnp.concatenate([
        pieces["qa"],
        k0, k0, k1, k1,
        v0, z64, z64, v0, v1, z64, z64, v1,
        pieces["cq"], pieces["ckv"],
        z64, kr, z32,
        z64, _rotate_half_cols(kr), z32,
    ], axis=1).astype(BF16)

    uq = w_uq[l].reshape(Q_LORA_RANK, N_HEADS_B, QK_NOPE_DIM + QK_ROPE_DIM)
    q_nope, q_rope = uq[..., :QK_NOPE_DIM], uq[..., QK_NOPE_DIM:]
    zq = jnp.zeros((Q_LORA_RANK, N_HEADS_B, SLOT - QK_NOPE_DIM - QK_ROPE_DIM), F32)
    zq64 = jnp.zeros((Q_LORA_RANK, N_HEADS_B, QK_NOPE_DIM), F32)
    q_main = jnp.concatenate([q_nope, q_rope, zq], axis=-1).reshape(Q_LORA_RANK, -1)
    q_rot = jnp.concatenate([zq64, _rotate_half_cols(q_rope), zq], axis=-1).reshape(Q_LORA_RANK, -1)
    wq = jnp.concatenate([q_main, q_rot], axis=1).astype(BF16)

    ukv = w_ukv[l].reshape(KV_LORA_RANK, N_HEADS_B, QK_NOPE_DIM + V_DIM_B)
    k_nope, v_b = ukv[..., :QK_NOPE_DIM], ukv[..., QK_NOPE_DIM:]
    zk = jnp.zeros((KV_LORA_RANK, N_HEADS_B, SLOT - QK_NOPE_DIM), F32)
    k_slots = jnp.concatenate([k_nope, zk], axis=-1).reshape(KV_LORA_RANK, -1)
    zv = jnp.zeros((KV_LORA_RANK, N_HEADS_B // 2, V_DIM_B), F32)
    v_even = jnp.concatenate([v_b[:, 0::2], zv], axis=-1)
    v_odd = jnp.concatenate([zv, v_b[:, 1::2]], axis=-1)
    v_slots = jnp.stack([v_even, v_odd], axis=2).reshape(KV_LORA_RANK, -1)
    wkv = jnp.concatenate([k_slots, v_slots], axis=1).astype(BF16)
    return w1, wq, wkv, pieces["ga"].astype(BF16), pieces["gb"].astype(BF16)


def _rope_slot_tables(seq):
    inv_freq = ROPE_THETA ** (-jnp.arange(0, QK_ROPE_DIM, 2, dtype=F32) / QK_ROPE_DIM)
    ang = jnp.arange(seq, dtype=F32)[:, None] * inv_freq[None, :]
    cos, sin = jnp.cos(ang), jnp.sin(ang)
    pad = jnp.zeros((seq, SLOT - QK_NOPE_DIM - QK_ROPE_DIM), F32)
    cos_tab = jnp.concatenate([jnp.ones((seq, QK_NOPE_DIM), F32), cos, cos, pad], axis=1)
    sin_tab = jnp.concatenate([jnp.zeros((seq, QK_NOPE_DIM), F32), sin, sin, pad], axis=1)
    return cos_tab, sin_tab


def _trunk(x3, bias, layers):
    batch, seq, _ = x3.shape
    x = x3.reshape(batch * seq, D_MODEL)
    cos_tab, sin_tab = _rope_slot_tables(seq)
    for lw in layers:
        x = _ffn(x, *lw["ffn1"])
        qa, ka, va, q, k, v = _proj(x, lw["mix_pre_g"], lw["w1"], lw["q_norm_g"], lw["kv_norm_g"],
                                    lw["wq"], lw["wkv"], cos_tab, sin_tab, seq)
        oa = _window_attention(lw["sink"], qa, ka, va, bias, batch, seq)
        ob = _mla_attention(q, k, v, batch, seq)
        x = _merge(x, oa, ob, lw["mix_pre_g"], lw["mix_post_g"], lw["wga"], lw["wgb"],
                   lw["w_a_out"], lw["w_b_out"], lw["w_o"])
        x = _ffn(x, *lw["ffn2"])
    return x.reshape(batch, seq, D_MODEL)


def kernel(x_prompt, x_sample, rel_bias, ffn1_pre_g, ffn1_post_g, ffn1_w_gate, ffn1_w_up, ffn1_w_down, mix_pre_g, mix_post_g, w_in, sink, q_norm_g, kv_norm_g, w_uq, w_ukv, w_a_out, w_b_out, w_o, ffn2_pre_g, ffn2_post_g, ffn2_w_gate, ffn2_w_up, ffn2_w_down):
    def vec(g, l):
        return g[l].reshape(1, -1)

    layers = []
    for l in range(DEPTH):
        w1, wq, wkv, wga, wgb = _layer_weights(l, w_in, w_uq, w_ukv)
        layers.append(dict(
            ffn1=(vec(ffn1_pre_g, l), vec(ffn1_post_g, l), ffn1_w_gate[l].astype(BF16),
                  ffn1_w_up[l].astype(BF16), ffn1_w_down[l].astype(BF16)),
            ffn2=(vec(ffn2_pre_g, l), vec(ffn2_post_g, l), ffn2_w_gate[l].astype(BF16),
                  ffn2_w_up[l].astype(BF16), ffn2_w_down[l].astype(BF16)),
            mix_pre_g=vec(mix_pre_g, l), mix_post_g=vec(mix_post_g, l),
            q_norm_g=vec(q_norm_g, l), kv_norm_g=vec(kv_norm_g, l),
            w1=w1, wq=wq, wkv=wkv, wga=wga, wgb=wgb, sink=sink[l],
            w_a_out=w_a_out[l].astype(BF16), w_b_out=w_b_out[l].astype(BF16),
            w_o=w_o[l].astype(BF16),
        ))
    bias = _bias_table(rel_bias, _band_buckets())
    return _trunk(x_prompt, bias, layers), _trunk(x_sample, bias, layers)
```

```python
import functools
import math

import jax
import jax.numpy as jnp
from jax import lax
from jax.experimental import pallas as pl
from jax.experimental.pallas import tpu as pltpu

F32 = jnp.float32
BF16 = jnp.bfloat16

D_MODEL = 1024
DEPTH = 2
N_HEADS_A = 8
N_KV_HEADS_A = 2
GROUP_A = N_HEADS_A // N_KV_HEADS_A
HEAD_DIM_A = 64
WINDOW = 128
BLOCK = 128
N_HEADS_B = 8
QK_NOPE_DIM = 64
QK_ROPE_DIM = 32
V_DIM_B = 64
Q_LORA_RANK = 384
KV_LORA_RANK = 256
ROPE_THETA = 10000.0
N_BUCKETS = 32
MAX_DISTANCE = 128
D_FF = 2816
FFN_RES_WEIGHT = 0.5
EPS = 1e-6

A_Q_COLS = N_HEADS_A * HEAD_DIM_A
A_KV_COLS = N_KV_HEADS_A * HEAD_DIM_A

LANES = 128
SLOT = LANES
VMEM_LIMIT_BYTES = 56 * 1024 * 1024

W1_QA = (0, 512)
W1_KA = (512, 768)
W1_VA = (768, 1024)
W1_CQ = (1024, 1408)
W1_CKV = (1408, 1664)
W1_KR = (1664, 1792)
W1_KRR = (1792, 1920)
W1_COLS = 1920

FFN_CHUNKS = ((0, 1024), (1024, 1024), (2048, 768))
FFN_TM = 256
PROJ_TM = 256
MERGE_TM = 256
MLA_TQ = 256
MLA_KC = 512
WIN_BLOCKS = 4


def _rms(x, g):
    return x * lax.rsqrt(jnp.mean(x * x, axis=-1, keepdims=True) + EPS) * g


def _const_spec(shape):
    nd = len(shape)
    return pl.BlockSpec(shape, lambda *_: (0,) * nd, pipeline_mode=pl.Buffered(1))


def _params(n_axes, flags=None):
    return pltpu.CompilerParams(
        dimension_semantics=("parallel",) * n_axes,
        vmem_limit_bytes=VMEM_LIMIT_BYTES,
        flags=flags,
    )


def _ffn_kernel(x_ref, pre_ref, post_ref, wg_ref, wu_ref, wd_ref, o_ref, acc_ref):
    x = x_ref[...]
    hb = _rms(x, pre_ref[...]).astype(BF16)
    for idx, (c0, cw) in enumerate(FFN_CHUNKS):
        g = jnp.dot(hb, wg_ref[:, c0:c0 + cw], preferred_element_type=F32)
        u = jnp.dot(hb, wu_ref[:, c0:c0 + cw], preferred_element_type=F32)
        a = (jax.nn.silu(g) * u).astype(BF16)
        part = jnp.dot(a, wd_ref[c0:c0 + cw, :], preferred_element_type=F32)
        if idx == 0:
            acc_ref[...] = part
        else:
            acc_ref[...] += part
    o_ref[...] = x + FFN_RES_WEIGHT * _rms(acc_ref[...], post_ref[...])


def _ffn(x, pre_g, post_g, wg, wu, wd):
    t = x.shape[0]
    tm = FFN_TM
    row = pl.BlockSpec((tm, D_MODEL), lambda i: (i, 0))
    return pl.pallas_call(
        _ffn_kernel,
        grid=(t // tm,),
        in_specs=[row, _const_spec((1, D_MODEL)), _const_spec((1, D_MODEL)),
                  _const_spec((D_MODEL, D_FF)), _const_spec((D_MODEL, D_FF)),
                  _const_spec((D_FF, D_MODEL))],
        out_specs=row,
        out_shape=jax.ShapeDtypeStruct((t, D_MODEL), F32),
        scratch_shapes=[pltpu.VMEM((tm, D_MODEL), F32)],
        compiler_params=_params(1),
        name="ffn",
    )(x, pre_g, post_g, wg, wu, wd)


def _proj_kernel(x_ref, g_ref, w1_ref, qg_ref, kvg_ref, wq_ref, wkv_ref, cos_ref, sin_ref,
                 qa_ref, ka_ref, va_ref, q_ref, k_ref, v_ref):
    hb = _rms(x_ref[...], g_ref[...]).astype(BF16)
    proj = jnp.dot(hb, w1_ref[...], preferred_element_type=F32)
    qa_ref[...] = proj[:, W1_QA[0]:W1_QA[1]].astype(BF16)
    ka_ref[...] = proj[:, W1_KA[0]:W1_KA[1]].astype(BF16)
    va_ref[...] = proj[:, W1_VA[0]:W1_VA[1]].astype(BF16)
    cqn = _rms(proj[:, W1_CQ[0]:W1_CQ[1]], qg_ref[...]).astype(BF16)
    ckvn = _rms(proj[:, W1_CKV[0]:W1_CKV[1]], kvg_ref[...]).astype(BF16)
    cos = cos_ref[...]
    sin = sin_ref[...]
    k_rope = proj[:, W1_KR[0]:W1_KR[1]] * cos + proj[:, W1_KRR[0]:W1_KRR[1]] * sin
    qf = jnp.dot(cqn, wq_ref[...], preferred_element_type=F32)
    kvf = jnp.dot(ckvn, wkv_ref[...], preferred_element_type=F32)
    rot0 = N_HEADS_B * SLOT
    for h in range(N_HEADS_B):
        lo, hi = h * SLOT, (h + 1) * SLOT
        q_ref[:, lo:hi] = (qf[:, lo:hi] * cos + qf[:, rot0 + lo:rot0 + hi] * sin).astype(BF16)
        k_ref[:, lo:hi] = (kvf[:, lo:hi] + k_rope).astype(BF16)
    v = kvf[:, rot0:]
    pair_lane = lax.broadcasted_iota(jnp.int32, v.shape, 1) % (2 * SLOT)
    v_ref[...] = jnp.where((pair_lane == V_DIM_B) | (pair_lane == SLOT), 1.0, v).astype(BF16)


def _proj(x, g, w1, qg, kvg, wq, wkv, cos_tab, sin_tab, seq):
    t = x.shape[0]
    tm = PROJ_TM
    nseq = seq // tm

    def row(cols):
        return pl.BlockSpec((tm, cols), lambda i: (i, 0))

    tab = pl.BlockSpec((tm, SLOT), lambda i: (i % nseq, 0))
    hs = N_HEADS_B * SLOT
    out_cols = (A_Q_COLS, W1_KA[1] - W1_KA[0], W1_VA[1] - W1_VA[0], hs, hs, hs)
    return pl.pallas_call(
        _proj_kernel,
        grid=(t // tm,),
        in_specs=[row(D_MODEL), _const_spec((1, D_MODEL)), _const_spec((D_MODEL, W1_COLS)),
                  _const_spec((1, Q_LORA_RANK)), _const_spec((1, KV_LORA_RANK)),
                  _const_spec((Q_LORA_RANK, 2 * hs)), _const_spec((KV_LORA_RANK, 2 * hs)),
                  tab, tab],
        out_specs=[row(c) for c in out_cols],
        out_shape=[jax.ShapeDtypeStruct((t, c), BF16) for c in out_cols],
        compiler_params=_params(1),
        name="proj",
    )(x, g, w1, qg, kvg, wq, wkv, cos_tab, sin_tab)


def _bias_kernel(rb_ref, bucket_ref, o_ref):
    h = pl.program_id(1)
    bucket = bucket_ref[0]
    acc = jnp.full(bucket.shape, -jnp.inf, F32)
    for b in range(N_BUCKETS):
        acc = jnp.where(bucket == b, rb_ref[b, h], acc)
    o_ref[0, 0] = acc


def _bias_table(rel_bias, bucket):
    kw = 3 * BLOCK
    return pl.pallas_call(
        _bias_kernel,
        grid=(3, N_HEADS_A),
        in_specs=[pl.BlockSpec(memory_space=pltpu.SMEM),
                  pl.BlockSpec((1, BLOCK, kw), lambda e, h: (e, 0, 0))],
        out_specs=pl.BlockSpec((1, 1, BLOCK, kw), lambda e, h: (e, h, 0, 0)),
        out_shape=jax.ShapeDtypeStruct((3, N_HEADS_A, BLOCK, kw), F32),
        compiler_params=_params(2),
        name="t5_bias",
    )(rel_bias, bucket)


def _t5_bucket(rel):
    half = N_BUCKETS // 2
    max_exact = half // 2
    ret = jnp.where(rel > 0, half, 0)
    n = jnp.abs(rel)
    nf = jnp.maximum(n, 1).astype(F32)
    large = max_exact + (jnp.log(nf / max_exact) / math.log(MAX_DISTANCE / max_exact)
                         * (half - max_exact)).astype(jnp.int32)
    large = jnp.minimum(large, half - 1)
    return ret + jnp.where(n < max_exact, n, large)


def _band_buckets():
    qi = jnp.arange(BLOCK, dtype=jnp.int32)[:, None]
    si = jnp.arange(3 * BLOCK, dtype=jnp.int32)[None, :]
    out = []
    for shift in (0, -BLOCK, -2 * BLOCK):
        rel = si + shift - qi
        out.append(jnp.where(jnp.abs(rel) <= WINDOW, _t5_bucket(rel), -1))
    return jnp.stack(out).astype(jnp.int32)


def _win_kernel(q_ref, k_ref, v_ref, bias_ref, sink_ref, o_ref, *, seq):
    step = pl.program_id(1)
    nb = seq // BLOCK
    kw = 3 * BLOCK
    gw = GROUP_A * HEAD_DIM_A
    lane = lax.broadcasted_iota(jnp.int32, (BLOCK, gw), 1)
    keep = [(lane >= r * HEAD_DIM_A) & (lane < (r + 1) * HEAD_DIM_A) for r in range(GROUP_A)]
    for blk in range(WIN_BLOCKS):
        n = step * WIN_BLOCKS + blk
        start = pl.multiple_of(jnp.clip((n - 1) * BLOCK, 0, seq - kw), BLOCK)
        edge = jnp.where(n == 0, 0, jnp.where(n == nb - 1, 2, 1))
        rows = slice(blk * BLOCK, (blk + 1) * BLOCK)
        for g in range(N_KV_HEADS_A):
            q_grp = q_ref[rows, g * gw:(g + 1) * gw]
            q_stack = jnp.concatenate(
                [jnp.where(keep[r], q_grp, jnp.zeros_like(q_grp)) for r in range(GROUP_A)], axis=0)
            k_dup = k_ref[pl.ds(start, kw), g * SLOT:(g + 1) * SLOT]
            v_dup = v_ref[pl.ds(start, kw), g * SLOT:(g + 1) * SLOT]
            k_rep = jnp.concatenate([k_dup, k_dup], axis=1)
            v_rep = jnp.concatenate([v_dup, v_dup], axis=1)
            s = lax.dot_general(q_stack, k_rep, (((1,), (1,)), ((), ())),
                                preferred_element_type=F32)
            s = jnp.concatenate([s + bias_ref[edge, g], sink_ref[g]], axis=1)
            p = jnp.exp(s - jnp.max(s, axis=-1, keepdims=True))
            denom = jnp.sum(p, axis=-1, keepdims=True)
            o_all = jnp.dot(p[:, :kw].astype(BF16), v_rep, preferred_element_type=F32) / denom
            out = jnp.zeros((BLOCK, gw), F32)
            for r in range(GROUP_A):
                out = jnp.where(keep[r], o_all[r * BLOCK:(r + 1) * BLOCK], out)
            o_ref[rows, g * gw:(g + 1) * gw] = out.astype(BF16)


def _window_attention(qa, ka, va, bias, sink_col, batch, seq):
    tq = WIN_BLOCKS * BLOCK
    kc = ka.shape[-1]
    vc = va.shape[-1]
    return pl.pallas_call(
        functools.partial(_win_kernel, seq=seq),
        grid=(batch, seq // tq),
        in_specs=[pl.BlockSpec((None, tq, A_Q_COLS), lambda b, i: (b, i, 0)),
                  pl.BlockSpec((None, seq, kc), lambda b, i: (b, 0, 0)),
                  pl.BlockSpec((None, seq, vc), lambda b, i: (b, 0, 0)),
                  _const_spec(bias.shape), _const_spec(sink_col.shape)],
        out_specs=pl.BlockSpec((None, tq, A_Q_COLS), lambda b, i: (b, i, 0)),
        out_shape=jax.ShapeDtypeStruct((batch, seq, A_Q_COLS), BF16),
        compiler_params=_params(2),
        name="window_attn",
    )(qa.reshape(batch, seq, -1), ka.reshape(batch, seq, kc), va.reshape(batch, seq, vc), bias, sink_col)


def _mla_kernel(q_ref, k_ref, v_ref, o_ref, s0_ref, s1_ref, *, seq):
    c2 = (QK_NOPE_DIM + QK_ROPE_DIM) ** -0.5 * math.log2(math.e)
    n_chunks = seq // MLA_KC
    tq = q_ref.shape[0]
    lane = lax.broadcasted_iota(jnp.int32, (tq, SLOT), 1)
    s_bufs = (s0_ref, s1_ref)
    row0 = pl.multiple_of(jnp.minimum(pl.program_id(1), 0), 8)
    tile_rows = pl.ds(row0, tq)

    def scores(h):
        lo, hi = h * SLOT, (h + 1) * SLOT
        q_h = q_ref[:, lo:hi]
        m = jnp.full((tq, 1), -jnp.inf, F32)
        for c in range(n_chunks):
            rows = slice(c * MLA_KC, (c + 1) * MLA_KC)
            s = lax.dot_general(q_h, k_ref[rows, lo:hi], (((1,), (1,)), ((), ())),
                                preferred_element_type=F32)
            s_bufs[h % 2][tile_rows, rows] = s
            m = jnp.maximum(m, jnp.max(s, axis=-1, keepdims=True))
        return m

    def weighted(h, m):
        lo, hi = h * SLOT, (h + 1) * SLOT
        acc = jnp.zeros((tq, SLOT), F32)
        for c in range(n_chunks):
            rows = slice(c * MLA_KC, (c + 1) * MLA_KC)
            p = jnp.exp2((s_bufs[h % 2][tile_rows, rows] - m) * c2)
            acc = acc + jnp.dot(p.astype(BF16), v_ref[rows, lo:hi], preferred_element_type=F32)
        return acc

    m = scores(0)
    even = None
    for h in range(N_HEADS_B):
        m_next = scores(h + 1) if h + 1 < N_HEADS_B else None
        acc = weighted(h, m)
        if h % 2 == 0:
            even = acc
        else:
            denom_even = even[:, V_DIM_B:V_DIM_B + 1]
            denom_odd = acc[:, 0:1]
            pair = jnp.where(lane < V_DIM_B, even / denom_even, acc / denom_odd)
            o_ref[:, (h // 2) * SLOT:(h // 2 + 1) * SLOT] = pair.astype(BF16)
        m = m_next


def _mla_attention(q, k, v, batch, seq):
    tq = MLA_TQ
    hs = N_HEADS_B * SLOT
    oc = N_HEADS_B * V_DIM_B
    return pl.pallas_call(
        functools.partial(_mla_kernel, seq=seq),
        grid=(batch, seq // tq),
        in_specs=[pl.BlockSpec((None, tq, hs), lambda b, i: (b, i, 0)),
                  pl.BlockSpec((None, seq, hs), lambda b, i: (b, 0, 0)),
                  pl.BlockSpec((None, seq, hs), lambda b, i: (b, 0, 0))],
        out_specs=pl.BlockSpec((None, tq, oc), lambda b, i: (b, i, 0)),
        out_shape=jax.ShapeDtypeStruct((batch, seq, oc), BF16),
        scratch_shapes=[pltpu.VMEM((tq, seq), F32), pltpu.VMEM((tq, seq), F32)],
        compiler_params=_params(2),
        name="mla_attn",
    )(q.reshape(batch, seq, hs), k.reshape(batch, seq, hs), v.reshape(batch, seq, hs))


def _merge_kernel(x_ref, oa_ref, ob_ref, pre_ref, post_ref, wga_ref, wgb_ref, wa_ref, wb_ref,
                  wo_ref, o_ref):
    x = x_ref[...]
    hb = _rms(x, pre_ref[...]).astype(BF16)
    ga = jax.nn.sigmoid(jnp.dot(hb, wga_ref[...], preferred_element_type=F32))
    merged = ga * jnp.dot(oa_ref[...], wa_ref[...], preferred_element_type=F32)
    gb = jax.nn.sigmoid(jnp.dot(hb, wgb_ref[...], preferred_element_type=F32))
    merged = merged + gb * jnp.dot(ob_ref[...], wb_ref[...], preferred_element_type=F32)
    y = jnp.dot(merged.astype(BF16), wo_ref[...], preferred_element_type=F32)
    o_ref[...] = x + _rms(y, post_ref[...])


def _merge(x, oa, ob, pre_g, post_g, wga, wgb, wa, wb, wo):
    t = x.shape[0]
    tm = MERGE_TM

    def row(cols):
        return pl.BlockSpec((tm, cols), lambda i: (i, 0))

    bcols = N_HEADS_B * V_DIM_B
    return pl.pallas_call(
        _merge_kernel,
        grid=(t // tm,),
        in_specs=[row(D_MODEL), row(A_Q_COLS), row(bcols),
                  _const_spec((1, D_MODEL)), _const_spec((1, D_MODEL)),
                  _const_spec((D_MODEL, D_MODEL)), _const_spec((D_MODEL, D_MODEL)),
                  _const_spec((A_Q_COLS, D_MODEL)), _const_spec((bcols, D_MODEL)),
                  _const_spec((D_MODEL, D_MODEL))],
        out_specs=row(D_MODEL),
        out_shape=jax.ShapeDtypeStruct((t, D_MODEL), F32),
        compiler_params=_params(1),
        name="merge",
    )(x, oa.reshape(t, -1), ob.reshape(t, -1), pre_g, post_g, wga, wgb, wa, wb, wo)


def _rotate_half_cols(w):
    half = w.shape[-1] // 2
    return jnp.concatenate([-w[..., half:], w[..., :half]], axis=-1)


def _layer_weights(l, w_in, w_uq, w_ukv):
    wi = w_in[l]
    o = 0
    pieces = {}
    for name, sz in (("qa", A_Q_COLS), ("ka", A_KV_COLS), ("va", A_KV_COLS), ("cq", Q_LORA_RANK),
                     ("ckv", KV_LORA_RANK), ("kr", QK_ROPE_DIM), ("ga", D_MODEL), ("gb", D_MODEL)):
        pieces[name] = wi[:, o:o + sz]
        o += sz
    z64 = jnp.zeros((D_MODEL, 64), F32)
    z32 = jnp.zeros((D_MODEL, 32), F32)
    ka = pieces["ka"]
    va = pieces["va"]
    k0, k1 = ka[:, :HEAD_DIM_A], ka[:, HEAD_DIM_A:]
    v0, v1 = va[:, :HEAD_DIM_A], va[:, HEAD_DIM_A:]
    kr = pieces["kr"]
    w1 = jnp.concatenate([
        pieces["qa"] * (HEAD_DIM_A ** -0.5),
        k0, k0, k1, k1,
        v0, v0, v1, v1,
        pieces["cq"], pieces["ckv"],
        z64, kr, z32,
        z64, _rotate_half_cols(kr), z32,
    ], axis=1).astype(BF16)

    uq = w_uq[l].reshape(Q_LORA_RANK, N_HEADS_B, QK_NOPE_DIM + QK_ROPE_DIM)
    q_nope, q_rope = uq[..., :QK_NOPE_DIM], uq[..., QK_NOPE_DIM:]
    zq = jnp.zeros((Q_LORA_RANK, N_HEADS_B, SLOT - QK_NOPE_DIM - QK_ROPE_DIM), F32)
    zq64 = jnp.zeros((Q_LORA_RANK, N_HEADS_B, QK_NOPE_DIM), F32)
    q_main = jnp.concatenate([q_nope, q_rope, zq], axis=-1).reshape(Q_LORA_RANK, -1)
    q_rot = jnp.concatenate([zq64, _rotate_half_cols(q_rope), zq], axis=-1).reshape(Q_LORA_RANK, -1)
    wq = jnp.concatenate([q_main, q_rot], axis=1).astype(BF16)

    ukv = w_ukv[l].reshape(KV_LORA_RANK, N_HEADS_B, QK_NOPE_DIM + V_DIM_B)
    k_nope, v_b = ukv[..., :QK_NOPE_DIM], ukv[..., QK_NOPE_DIM:]
    zk = jnp.zeros((KV_LORA_RANK, N_HEADS_B, SLOT - QK_NOPE_DIM), F32)
    k_slots = jnp.concatenate([k_nope, zk], axis=-1).reshape(KV_LORA_RANK, -1)
    zv = jnp.zeros((KV_LORA_RANK, N_HEADS_B // 2, V_DIM_B), F32)
    v_even = jnp.concatenate([v_b[:, 0::2], zv], axis=-1)
    v_odd = jnp.concatenate([zv, v_b[:, 1::2]], axis=-1)
    v_slots = jnp.stack([v_even, v_odd], axis=2).reshape(KV_LORA_RANK, -1)
    wkv = jnp.concatenate([k_slots, v_slots], axis=1).astype(BF16)
    return w1, wq, wkv, pieces["ga"].astype(BF16), pieces["gb"].astype(BF16)


def _sink_block(sink_l):
    col = jnp.repeat(sink_l, BLOCK).reshape(N_KV_HEADS_A, GROUP_A * BLOCK, 1)
    lane = jnp.arange(SLOT)[None, None, :]
    return jnp.where(lane == 0, col, -jnp.inf).astype(F32)


def _rope_slot_tables(seq):
    inv_freq = ROPE_THETA ** (-jnp.arange(0, QK_ROPE_DIM, 2, dtype=F32) / QK_ROPE_DIM)
    ang = jnp.arange(seq, dtype=F32)[:, None] * inv_freq[None, :]
    cos, sin = jnp.cos(ang), jnp.sin(ang)
    pad = jnp.zeros((seq, SLOT - QK_NOPE_DIM - QK_ROPE_DIM), F32)
    cos_tab = jnp.concatenate([jnp.ones((seq, QK_NOPE_DIM), F32), cos, cos, pad], axis=1)
    sin_tab = jnp.concatenate([jnp.zeros((seq, QK_NOPE_DIM), F32), sin, sin, pad], axis=1)
    return cos_tab, sin_tab


def _trunk(x3, bias, layers):
    batch, seq, _ = x3.shape
    x = x3.reshape(batch * seq, D_MODEL)
    cos_tab, sin_tab = _rope_slot_tables(seq)
    for lw in layers:
        x = _ffn(x, *lw["ffn1"])
        qa, ka, va, q, k, v = _proj(x, lw["mix_pre_g"], lw["w1"], lw["q_norm_g"], lw["kv_norm_g"],
                                    lw["wq"], lw["wkv"], cos_tab, sin_tab, seq)
        oa = _window_attention(qa, ka, va, bias, lw["sink_col"], batch, seq)
        ob = _mla_attention(q, k, v, batch, seq)
        x = _merge(x, oa, ob, lw["mix_pre_g"], lw["mix_post_g"], lw["wga"], lw["wgb"],
                   lw["w_a_out"], lw["w_b_out"], lw["w_o"])
        x = _ffn(x, *lw["ffn2"])
    return x.reshape(batch, seq, D_MODEL)


def kernel(x_prompt, x_sample, rel_bias, ffn1_pre_g, ffn1_post_g, ffn1_w_gate, ffn1_w_up, ffn1_w_down, mix_pre_g, mix_post_g, w_in, sink, q_norm_g, kv_norm_g, w_uq, w_ukv, w_a_out, w_b_out, w_o, ffn2_pre_g, ffn2_post_g, ffn2_w_gate, ffn2_w_up, ffn2_w_down):
    def vec(g, l):
        return g[l].reshape(1, -1)

    layers = []
    for l in range(DEPTH):
        w1, wq, wkv, wga, wgb = _layer_weights(l, w_in, w_uq, w_ukv)
        layers.append(dict(
            ffn1=(vec(ffn1_pre_g, l), vec(ffn1_post_g, l), ffn1_w_gate[l].astype(BF16),
                  ffn1_w_up[l].astype(BF16), ffn1_w_down[l].astype(BF16)),
            ffn2=(vec(ffn2_pre_g, l), vec(ffn2_post_g, l), ffn2_w_gate[l].astype(BF16),
                  ffn2_w_up[l].astype(BF16), ffn2_w_down[l].astype(BF16)),
            mix_pre_g=vec(mix_pre_g, l), mix_post_g=vec(mix_post_g, l),
            q_norm_g=vec(q_norm_g, l), kv_norm_g=vec(kv_norm_g, l),
            w1=w1, wq=wq, wkv=wkv, wga=wga, wgb=wgb,
            sink_col=_sink_block(sink[l]),
            w_a_out=w_a_out[l].astype(BF16), w_b_out=w_b_out[l].astype(BF16),
            w_o=w_o[l].astype(BF16),
        ))
    bias = _bias_table(rel_bias, _band_buckets())
    bias = bias.reshape(3, N_KV_HEADS_A, GROUP_A * BLOCK, 3 * BLOCK)
    return _trunk(x_prompt, bias, layers), _trunk(x_sample, bias, layers)
```

```python
import functools
import math

import jax
import jax.numpy as jnp
from jax import lax
from jax.experimental import pallas as pl
from jax.experimental.pallas import tpu as pltpu

F32 = jnp.float32
BF16 = jnp.bfloat16

D_MODEL = 1024
DEPTH = 2
N_HEADS_A = 8
N_KV_HEADS_A = 2
GROUP_A = N_HEADS_A // N_KV_HEADS_A
HEAD_DIM_A = 64
WINDOW = 128
BLOCK = 128
N_HEADS_B = 8
QK_NOPE_DIM = 64
QK_ROPE_DIM = 32
V_DIM_B = 64
Q_LORA_RANK = 384
KV_LORA_RANK = 256
ROPE_THETA = 10000.0
N_BUCKETS = 32
MAX_DISTANCE = 128
D_FF = 2816
FFN_RES_WEIGHT = 0.5
EPS = 1e-6

A_Q_COLS = N_HEADS_A * HEAD_DIM_A
A_KV_COLS = N_KV_HEADS_A * HEAD_DIM_A

LANES = 128
SLOT = LANES
VMEM_LIMIT_BYTES = 56 * 1024 * 1024

W1_QA = (0, 512)
W1_KA = (512, 768)
W1_VA = (768, 1024)
W1_CQ = (1024, 1408)
W1_CKV = (1408, 1664)
W1_KR = (1664, 1792)
W1_COLS = 1792

FFN_CHUNKS = ((0, 1024), (1024, 1024), (2048, 768))
ROW_SUB = 256
FFN_TM = 512
PROJ_TM = 512
MERGE_TM = 512
MLA_TQ = 256
MLA_KC = 512
WIN_BLOCKS = 4


def _rms(x, g):
    return x * lax.rsqrt(jnp.mean(x * x, axis=-1, keepdims=True) + EPS) * g


def _const_spec(shape):
    nd = len(shape)
    return pl.BlockSpec(shape, lambda *_: (0,) * nd, pipeline_mode=pl.Buffered(1))


def _params(n_axes, flags=None):
    return pltpu.CompilerParams(
        dimension_semantics=("parallel",) * n_axes,
        vmem_limit_bytes=VMEM_LIMIT_BYTES,
        flags=flags,
    )


def _ffn_kernel(x_ref, pre_ref, post_ref, wg_ref, wu_ref, wd_ref, o_ref, acc_ref):
    for sub in range(x_ref.shape[0] // ROW_SUB):
        rows = slice(sub * ROW_SUB, (sub + 1) * ROW_SUB)
        x = x_ref[rows, :]
        hb = _rms(x, pre_ref[...]).astype(BF16)
        for idx, (c0, cw) in enumerate(FFN_CHUNKS):
            g = jnp.dot(hb, wg_ref[:, c0:c0 + cw], preferred_element_type=F32)
            u = jnp.dot(hb, wu_ref[:, c0:c0 + cw], preferred_element_type=F32)
            a = (jax.nn.silu(g) * u).astype(BF16)
            part = jnp.dot(a, wd_ref[c0:c0 + cw, :], preferred_element_type=F32)
            if idx == 0:
                acc_ref[rows, :] = part
            else:
                acc_ref[rows, :] += part
        o_ref[rows, :] = x + FFN_RES_WEIGHT * _rms(acc_ref[rows, :], post_ref[...])


def _ffn(x, pre_g, post_g, wg, wu, wd):
    t = x.shape[0]
    tm = FFN_TM
    row = pl.BlockSpec((tm, D_MODEL), lambda i: (i, 0))
    return pl.pallas_call(
        _ffn_kernel,
        grid=(t // tm,),
        in_specs=[row, _const_spec((1, D_MODEL)), _const_spec((1, D_MODEL)),
                  _const_spec((D_MODEL, D_FF)), _const_spec((D_MODEL, D_FF)),
                  _const_spec((D_FF, D_MODEL))],
        out_specs=row,
        out_shape=jax.ShapeDtypeStruct((t, D_MODEL), F32),
        scratch_shapes=[pltpu.VMEM((tm, D_MODEL), F32)],
        compiler_params=_params(1),
        name="ffn",
    )(x, pre_g, post_g, wg, wu, wd)


def _rope(x, cos, sin_lo, sin_hi):
    half = QK_ROPE_DIM // 2
    return x * cos + pltpu.roll(x, SLOT - half, 1) * sin_lo + pltpu.roll(x, half, 1) * sin_hi


def _proj_kernel(x_ref, g_ref, w1_ref, qg_ref, kvg_ref, wq_ref, wkv_ref, cos_ref, sin_lo_ref, sin_hi_ref,
                 qa_ref, ka_ref, va_ref, q_ref, k_ref, v_ref):
    hs = N_HEADS_B * SLOT
    for sub in range(x_ref.shape[0] // ROW_SUB):
        rows = slice(sub * ROW_SUB, (sub + 1) * ROW_SUB)
        hb = _rms(x_ref[rows, :], g_ref[...]).astype(BF16)
        proj = jnp.dot(hb, w1_ref[...], preferred_element_type=F32)
        qa_ref[rows, :] = proj[:, W1_QA[0]:W1_QA[1]].astype(BF16)
        ka_ref[rows, :] = proj[:, W1_KA[0]:W1_KA[1]].astype(BF16)
        va_ref[rows, :] = proj[:, W1_VA[0]:W1_VA[1]].astype(BF16)
        cqn = _rms(proj[:, W1_CQ[0]:W1_CQ[1]], qg_ref[...]).astype(BF16)
        ckvn = _rms(proj[:, W1_CKV[0]:W1_CKV[1]], kvg_ref[...]).astype(BF16)
        cos = cos_ref[rows, :]
        sin_lo = sin_lo_ref[rows, :]
        sin_hi = sin_hi_ref[rows, :]
        k_rope = _rope(proj[:, W1_KR[0]:W1_KR[1]], cos, sin_lo, sin_hi)
        qf = jnp.dot(cqn, wq_ref[...], preferred_element_type=F32)
        kvf = jnp.dot(ckvn, wkv_ref[...], preferred_element_type=F32)
        for h in range(N_HEADS_B):
            lo, hi = h * SLOT, (h + 1) * SLOT
            q_ref[rows, lo:hi] = _rope(qf[:, lo:hi], cos, sin_lo, sin_hi).astype(BF16)
            k_ref[rows, lo:hi] = (kvf[:, lo:hi] + k_rope).astype(BF16)
        v = kvf[:, hs:]
        pair_lane = lax.broadcasted_iota(jnp.int32, v.shape, 1) % (2 * SLOT)
        v_ref[rows, :] = jnp.where((pair_lane == V_DIM_B) | (pair_lane == SLOT), 1.0, v).astype(BF16)


def _proj(x, g, w1, qg, kvg, wq, wkv, rope_tabs, seq):
    t = x.shape[0]
    tm = PROJ_TM
    nseq = seq // tm

    def row(cols):
        return pl.BlockSpec((tm, cols), lambda i: (i, 0))

    tab = pl.BlockSpec((tm, SLOT), lambda i: (i % nseq, 0))
    hs = N_HEADS_B * SLOT
    out_cols = (A_Q_COLS, W1_KA[1] - W1_KA[0], W1_VA[1] - W1_VA[0], hs, hs, hs)
    return pl.pallas_call(
        _proj_kernel,
        grid=(t // tm,),
        in_specs=[row(D_MODEL), _const_spec((1, D_MODEL)), _const_spec((D_MODEL, W1_COLS)),
                  _const_spec((1, Q_LORA_RANK)), _const_spec((1, KV_LORA_RANK)),
                  _const_spec((Q_LORA_RANK, hs)), _const_spec((KV_LORA_RANK, 2 * hs)),
                  tab, tab, tab],
        out_specs=[row(c) for c in out_cols],
        out_shape=[jax.ShapeDtypeStruct((t, c), BF16) for c in out_cols],
        compiler_params=_params(1),
        name="proj",
    )(x, g, w1, qg, kvg, wq, wkv, *rope_tabs)


def _bias_kernel(rb_ref, bucket_ref, o_ref):
    h = pl.program_id(1)
    bucket = bucket_ref[0]
    acc = jnp.full(bucket.shape, -jnp.inf, F32)
    for b in range(N_BUCKETS):
        acc = jnp.where(bucket == b, rb_ref[b, h], acc)
    o_ref[0, 0] = acc


def _bias_table(rel_bias, bucket):
    kw = 3 * BLOCK
    return pl.pallas_call(
        _bias_kernel,
        grid=(3, N_HEADS_A),
        in_specs=[pl.BlockSpec(memory_space=pltpu.SMEM),
                  pl.BlockSpec((1, BLOCK, kw), lambda e, h: (e, 0, 0))],
        out_specs=pl.BlockSpec((1, 1, BLOCK, kw), lambda e, h: (e, h, 0, 0)),
        out_shape=jax.ShapeDtypeStruct((3, N_HEADS_A, BLOCK, kw), F32),
        compiler_params=_params(2),
        name="t5_bias",
    )(rel_bias, bucket)


def _t5_bucket(rel):
    half = N_BUCKETS // 2
    max_exact = half // 2
    ret = jnp.where(rel > 0, half, 0)
    n = jnp.abs(rel)
    nf = jnp.maximum(n, 1).astype(F32)
    large = max_exact + (jnp.log(nf / max_exact) / math.log(MAX_DISTANCE / max_exact)
                         * (half - max_exact)).astype(jnp.int32)
    large = jnp.minimum(large, half - 1)
    return ret + jnp.where(n < max_exact, n, large)


def _band_buckets():
    qi = jnp.arange(BLOCK, dtype=jnp.int32)[:, None]
    si = jnp.arange(3 * BLOCK, dtype=jnp.int32)[None, :]
    out = []
    for shift in (0, -BLOCK, -2 * BLOCK):
        rel = si + shift - qi
        out.append(jnp.where(jnp.abs(rel) <= WINDOW, _t5_bucket(rel), -1))
    return jnp.stack(out).astype(jnp.int32)


def _win_kernel(q_ref, k_ref, v_ref, bias_ref, sink_ref, o_ref, *, seq):
    step = pl.program_id(1)
    nb = seq // BLOCK
    kw = 3 * BLOCK
    gw = GROUP_A * HEAD_DIM_A
    lane = lax.broadcasted_iota(jnp.int32, (BLOCK, gw), 1)
    keep = [(lane >= r * HEAD_DIM_A) & (lane < (r + 1) * HEAD_DIM_A) for r in range(GROUP_A)]
    for blk in range(WIN_BLOCKS):
        n = step * WIN_BLOCKS + blk
        start = pl.multiple_of(jnp.clip((n - 1) * BLOCK, 0, seq - kw), BLOCK)
        edge = jnp.where(n == 0, 0, jnp.where(n == nb - 1, 2, 1))
        rows = slice(blk * BLOCK, (blk + 1) * BLOCK)
        for g in range(N_KV_HEADS_A):
            q_grp = q_ref[rows, g * gw:(g + 1) * gw]
            q_stack = jnp.concatenate(
                [jnp.where(keep[r], q_grp, jnp.zeros_like(q_grp)) for r in range(GROUP_A)], axis=0)
            k_dup = k_ref[pl.ds(start, kw), g * SLOT:(g + 1) * SLOT]
            v_dup = v_ref[pl.ds(start, kw), g * SLOT:(g + 1) * SLOT]
            k_rep = jnp.concatenate([k_dup, k_dup], axis=1)
            v_rep = jnp.concatenate([v_dup, v_dup], axis=1)
            s = lax.dot_general(q_stack, k_rep, (((1,), (1,)), ((), ())),
                                preferred_element_type=F32)
            s = jnp.concatenate([s + bias_ref[edge, g], sink_ref[g]], axis=1)
            p = jnp.exp(s - jnp.max(s, axis=-1, keepdims=True))
            denom = jnp.sum(p, axis=-1, keepdims=True)
            o_all = jnp.dot(p[:, :kw].astype(BF16), v_rep, preferred_element_type=F32) / denom
            out = jnp.zeros((BLOCK, gw), F32)
            for r in range(GROUP_A):
                out = jnp.where(keep[r], o_all[r * BLOCK:(r + 1) * BLOCK], out)
            o_ref[rows, g * gw:(g + 1) * gw] = out.astype(BF16)


def _window_attention(qa, ka, va, bias, sink_col, batch, seq):
    tq = WIN_BLOCKS * BLOCK
    kc = ka.shape[-1]
    vc = va.shape[-1]
    return pl.pallas_call(
        functools.partial(_win_kernel, seq=seq),
        grid=(batch, seq // tq),
        in_specs=[pl.BlockSpec((None, tq, A_Q_COLS), lambda b, i: (b, i, 0)),
                  pl.BlockSpec((None, seq, kc), lambda b, i: (b, 0, 0)),
                  pl.BlockSpec((None, seq, vc), lambda b, i: (b, 0, 0)),
                  _const_spec(bias.shape), _const_spec(sink_col.shape)],
        out_specs=pl.BlockSpec((None, tq, A_Q_COLS), lambda b, i: (b, i, 0)),
        out_shape=jax.ShapeDtypeStruct((batch, seq, A_Q_COLS), BF16),
        compiler_params=_params(2),
        name="window_attn",
    )(qa.reshape(batch, seq, -1), ka.reshape(batch, seq, kc), va.reshape(batch, seq, vc), bias, sink_col)


def _mla_kernel(q_ref, k_ref, v_ref, o_ref, s0_ref, s1_ref, *, seq):
    c2 = (QK_NOPE_DIM + QK_ROPE_DIM) ** -0.5 * math.log2(math.e)
    n_chunks = seq // MLA_KC
    tq = q_ref.shape[0]
    lane = lax.broadcasted_iota(jnp.int32, (tq, SLOT), 1)
    s_bufs = (s0_ref, s1_ref)
    row0 = pl.multiple_of(jnp.minimum(pl.program_id(1), 0), 16)
    tile_rows = pl.ds(row0, tq)

    def scores(h):
        lo, hi = h * SLOT, (h + 1) * SLOT
        q_h = q_ref[:, lo:hi]
        m = jnp.full((tq, 1), -jnp.inf, F32)
        for c in range(n_chunks):
            rows = slice(c * MLA_KC, (c + 1) * MLA_KC)
            s = lax.dot_general(q_h, k_ref[rows, lo:hi], (((1,), (1,)), ((), ())),
                                preferred_element_type=F32)
            s_bufs[h % 2][tile_rows, rows] = s
            m = jnp.maximum(m, jnp.max(s, axis=-1, keepdims=True))
        return m

    def weighted(h, m):
        lo = (h // 2) * 2 * SLOT
        acc = jnp.zeros((tq, 2 * SLOT), F32)
        for c in range(n_chunks):
            rows = slice(c * MLA_KC, (c + 1) * MLA_KC)
            p = jnp.exp2((s_bufs[h % 2][tile_rows, rows] - m) * c2)
            acc = acc + jnp.dot(p.astype(BF16), v_ref[rows, lo:lo + 2 * SLOT],
                                preferred_element_type=F32)
        return acc[:, (h % 2) * SLOT:(h % 2 + 1) * SLOT]

    m = scores(0)
    even = None
    for h in range(N_HEADS_B):
        m_next = scores(h + 1) if h + 1 < N_HEADS_B else None
        acc = weighted(h, m)
        if h % 2 == 0:
            even = acc
        else:
            denom_even = even[:, V_DIM_B:V_DIM_B + 1]
            denom_odd = acc[:, 0:1]
            pair = jnp.where(lane < V_DIM_B, even / denom_even, acc / denom_odd)
            o_ref[:, (h // 2) * SLOT:(h // 2 + 1) * SLOT] = pair.astype(BF16)
        m = m_next


def _mla_attention(q, k, v, batch, seq):
    tq = MLA_TQ
    hs = N_HEADS_B * SLOT
    oc = N_HEADS_B * V_DIM_B
    return pl.pallas_call(
        functools.partial(_mla_kernel, seq=seq),
        grid=(batch, seq // tq),
        in_specs=[pl.BlockSpec((None, tq, hs), lambda b, i: (b, i, 0)),
                  pl.BlockSpec((None, seq, hs), lambda b, i: (b, 0, 0)),
                  pl.BlockSpec((None, seq, hs), lambda b, i: (b, 0, 0))],
        out_specs=pl.BlockSpec((None, tq, oc), lambda b, i: (b, i, 0)),
        out_shape=jax.ShapeDtypeStruct((batch, seq, oc), BF16),
        scratch_shapes=[pltpu.VMEM((tq, seq), F32), pltpu.VMEM((tq, seq), F32)],
        compiler_params=_params(2),
        name="mla_attn",
    )(q.reshape(batch, seq, hs), k.reshape(batch, seq, hs), v.reshape(batch, seq, hs))


def _merge_kernel(x_ref, oa_ref, ob_ref, pre_ref, post_ref, wga_ref, wgb_ref, wa_ref, wb_ref,
                  wo_ref, o_ref):
    for sub in range(x_ref.shape[0] // ROW_SUB):
        rows = slice(sub * ROW_SUB, (sub + 1) * ROW_SUB)
        x = x_ref[rows, :]
        hb = _rms(x, pre_ref[...]).astype(BF16)
        ga = jax.nn.sigmoid(jnp.dot(hb, wga_ref[...], preferred_element_type=F32))
        merged = ga * jnp.dot(oa_ref[rows, :], wa_ref[...], preferred_element_type=F32)
        gb = jax.nn.sigmoid(jnp.dot(hb, wgb_ref[...], preferred_element_type=F32))
        merged = merged + gb * jnp.dot(ob_ref[rows, :], wb_ref[...], preferred_element_type=F32)
        y = jnp.dot(merged.astype(BF16), wo_ref[...], preferred_element_type=F32)
        o_ref[rows, :] = x + _rms(y, post_ref[...])


def _merge(x, oa, ob, pre_g, post_g, wga, wgb, wa, wb, wo):
    t = x.shape[0]
    tm = MERGE_TM

    def row(cols):
        return pl.BlockSpec((tm, cols), lambda i: (i, 0))

    bcols = N_HEADS_B * V_DIM_B
    return pl.pallas_call(
        _merge_kernel,
        grid=(t // tm,),
        in_specs=[row(D_MODEL), row(A_Q_COLS), row(bcols),
                  _const_spec((1, D_MODEL)), _const_spec((1, D_MODEL)),
                  _const_spec((D_MODEL, D_MODEL)), _const_spec((D_MODEL, D_MODEL)),
                  _const_spec((A_Q_COLS, D_MODEL)), _const_spec((bcols, D_MODEL)),
                  _const_spec((D_MODEL, D_MODEL))],
        out_specs=row(D_MODEL),
        out_shape=jax.ShapeDtypeStruct((t, D_MODEL), F32),
        compiler_params=_params(1),
        name="merge",
    )(x, oa.reshape(t, -1), ob.reshape(t, -1), pre_g, post_g, wga, wgb, wa, wb, wo)


def _layer_weights(l, w_in, w_uq, w_ukv):
    wi = w_in[l]
    o = 0
    pieces = {}
    for name, sz in (("qa", A_Q_COLS), ("ka", A_KV_COLS), ("va", A_KV_COLS), ("cq", Q_LORA_RANK),
                     ("ckv", KV_LORA_RANK), ("kr", QK_ROPE_DIM), ("ga", D_MODEL), ("gb", D_MODEL)):
        pieces[name] = wi[:, o:o + sz]
        o += sz
    z64 = jnp.zeros((D_MODEL, 64), F32)
    z32 = jnp.zeros((D_MODEL, 32), F32)
    ka = pieces["ka"]
    va = pieces["va"]
    k0, k1 = ka[:, :HEAD_DIM_A], ka[:, HEAD_DIM_A:]
    v0, v1 = va[:, :HEAD_DIM_A], va[:, HEAD_DIM_A:]
    kr = pieces["kr"]
    w1 = jnp.concatenate([
        pieces["qa"] * (HEAD_DIM_A ** -0.5),
        k0, k0, k1, k1,
        v0, v0, v1, v1,
        pieces["cq"], pieces["ckv"],
        z64, kr, z32,
    ], axis=1).astype(BF16)

    uq = w_uq[l].reshape(Q_LORA_RANK, N_HEADS_B, QK_NOPE_DIM + QK_ROPE_DIM)
    q_nope, q_rope = uq[..., :QK_NOPE_DIM], uq[..., QK_NOPE_DIM:]
    zq = jnp.zeros((Q_LORA_RANK, N_HEADS_B, SLOT - QK_NOPE_DIM - QK_ROPE_DIM), F32)
    wq = jnp.concatenate([q_nope, q_rope, zq], axis=-1).reshape(Q_LORA_RANK, -1).astype(BF16)

    ukv = w_ukv[l].reshape(KV_LORA_RANK, N_HEADS_B, QK_NOPE_DIM + V_DIM_B)
    k_nope, v_b = ukv[..., :QK_NOPE_DIM], ukv[..., QK_NOPE_DIM:]
    zk = jnp.zeros((KV_LORA_RANK, N_HEADS_B, SLOT - QK_NOPE_DIM), F32)
    k_slots = jnp.concatenate([k_nope, zk], axis=-1).reshape(KV_LORA_RANK, -1)
    zv = jnp.zeros((KV_LORA_RANK, N_HEADS_B // 2, V_DIM_B), F32)
    v_even = jnp.concatenate([v_b[:, 0::2], zv], axis=-1)
    v_odd = jnp.concatenate([zv, v_b[:, 1::2]], axis=-1)
    v_slots = jnp.stack([v_even, v_odd], axis=2).reshape(KV_LORA_RANK, -1)
    wkv = jnp.concatenate([k_slots, v_slots], axis=1).astype(BF16)
    return w1, wq, wkv, pieces["ga"].astype(BF16), pieces["gb"].astype(BF16)


def _sink_block(sink_l):
    col = jnp.repeat(sink_l, BLOCK).reshape(N_KV_HEADS_A, GROUP_A * BLOCK, 1)
    lane = jnp.arange(SLOT)[None, None, :]
    return jnp.where(lane == 0, col, -jnp.inf).astype(F32)


def _rope_slot_tables(seq):
    half = QK_ROPE_DIM // 2
    inv_freq = ROPE_THETA ** (-jnp.arange(0, QK_ROPE_DIM, 2, dtype=F32) / QK_ROPE_DIM)
    ang = jnp.arange(seq, dtype=F32)[:, None] * inv_freq[None, :]
    cos, sin = jnp.cos(ang), jnp.sin(ang)
    z_half = jnp.zeros((seq, half), F32)
    z_nope = jnp.zeros((seq, QK_NOPE_DIM), F32)
    pad = jnp.zeros((seq, SLOT - QK_NOPE_DIM - QK_ROPE_DIM), F32)
    cos_tab = jnp.concatenate([jnp.ones((seq, QK_NOPE_DIM), F32), cos, cos, pad], axis=1)
    sin_lo = jnp.concatenate([z_nope, -sin, z_half, pad], axis=1)
    sin_hi = jnp.concatenate([z_nope, z_half, sin, pad], axis=1)
    return cos_tab, sin_lo, sin_hi


def _trunk(x3, bias, layers):
    batch, seq, _ = x3.shape
    x = x3.reshape(batch * seq, D_MODEL)
    rope_tabs = _rope_slot_tables(seq)
    for lw in layers:
        x = _ffn(x, *lw["ffn1"])
        qa, ka, va, q, k, v = _proj(x, lw["mix_pre_g"], lw["w1"], lw["q_norm_g"], lw["kv_norm_g"],
                                    lw["wq"], lw["wkv"], rope_tabs, seq)
        oa = _window_attention(qa, ka, va, bias, lw["sink_col"], batch, seq)
        ob = _mla_attention(q, k, v, batch, seq)
        x = _merge(x, oa, ob, lw["mix_pre_g"], lw["mix_post_g"], lw["wga"], lw["wgb"],
                   lw["w_a_out"], lw["w_b_out"], lw["w_o"])
        x = _ffn(x, *lw["ffn2"])
    return x.reshape(batch, seq, D_MODEL)


def kernel(x_prompt, x_sample, rel_bias, ffn1_pre_g, ffn1_post_g, ffn1_w_gate, ffn1_w_up, ffn1_w_down, mix_pre_g, mix_post_g, w_in, sink, q_norm_g, kv_norm_g, w_uq, w_ukv, w_a_out, w_b_out, w_o, ffn2_pre_g, ffn2_post_g, ffn2_w_gate, ffn2_w_up, ffn2_w_down):
    def vec(g, l):
        return g[l].reshape(1, -1)

    layers = []
    for l in range(DEPTH):
        w1, wq, wkv, wga, wgb = _layer_weights(l, w_in, w_uq, w_ukv)
        layers.append(dict(
            ffn1=(vec(ffn1_pre_g, l), vec(ffn1_post_g, l), ffn1_w_gate[l].astype(BF16),
                  ffn1_w_up[l].astype(BF16), ffn1_w_down[l].astype(BF16)),
            ffn2=(vec(ffn2_pre_g, l), vec(ffn2_post_g, l), ffn2_w_gate[l].astype(BF16),
                  ffn2_w_up[l].astype(BF16), ffn2_w_down[l].astype(BF16)),
            mix_pre_g=vec(mix_pre_g, l), mix_post_g=vec(mix_post_g, l),
            q_norm_g=vec(q_norm_g, l), kv_norm_g=vec(kv_norm_g, l),
            w1=w1, wq=wq, wkv=wkv, wga=wga, wgb=wgb,
            sink_col=_sink_block(sink[l]),
            w_a_out=w_a_out[l].astype(BF16), w_b_out=w_b_out[l].astype(BF16),
            w_o=w_o[l].astype(BF16),
        ))
    bias = _bias_table(rel_bias, _band_buckets())
    bias = bias.reshape(3, N_KV_HEADS_A, GROUP_A * BLOCK, 3 * BLOCK)
    return _trunk(x_prompt, bias, layers), _trunk(x_sample, bias, layers)
```

```python
import functools
import math

import jax
import jax.numpy as jnp
from jax import lax
from jax.experimental import pallas as pl
from jax.experimental.pallas import tpu as pltpu

F32 = jnp.float32
BF16 = jnp.bfloat16

D_MODEL = 1024
DEPTH = 2
N_HEADS_A = 8
N_KV_HEADS_A = 2
GROUP_A = N_HEADS_A // N_KV_HEADS_A
HEAD_DIM_A = 64
WINDOW = 128
BLOCK = 128
N_HEADS_B = 8
QK_NOPE_DIM = 64
QK_ROPE_DIM = 32
V_DIM_B = 64
Q_LORA_RANK = 384
KV_LORA_RANK = 256
ROPE_THETA = 10000.0
N_BUCKETS = 32
MAX_DISTANCE = 128
D_FF = 2816
FFN_RES_WEIGHT = 0.5
EPS = 1e-6

A_Q_COLS = N_HEADS_A * HEAD_DIM_A
A_KV_COLS = N_KV_HEADS_A * HEAD_DIM_A

LANES = 128
SLOT = LANES
VMEM_LIMIT_BYTES = 56 * 1024 * 1024

W1_QA = (0, 512)
W1_KA = (512, 768)
W1_VA = (768, 1024)
W1_CQ = (1024, 1408)
W1_CKV = (1408, 1664)
W1_KR = (1664, 1792)
W1_COLS = 1792

FFN_CHUNKS = ((0, 1024), (1024, 1024), (2048, 768))
ROW_SUB = 256
FFN_TM = 1024
PROJ_TM = 1024
MERGE_TM = 1024
MLA_TQ = 256
MLA_KC = 512
WIN_BLOCKS = 4


def _rms(x, g):
    return x * lax.rsqrt(jnp.mean(x * x, axis=-1, keepdims=True) + EPS) * g


def _const_spec(shape):
    nd = len(shape)
    return pl.BlockSpec(shape, lambda *_: (0,) * nd, pipeline_mode=pl.Buffered(1))


def _layer_spec(shape, layer):
    nd = len(shape)
    return pl.BlockSpec((None,) + tuple(shape), lambda *_: (layer,) + (0,) * nd,
                        pipeline_mode=pl.Buffered(1))


def _params(n_axes, flags=None):
    return pltpu.CompilerParams(
        dimension_semantics=("parallel",) * n_axes,
        vmem_limit_bytes=VMEM_LIMIT_BYTES,
        flags=flags,
    )


def _ffn_kernel(x_ref, pre_ref, post_ref, wg_ref, wu_ref, wd_ref, o_ref, acc_ref):
    for sub in range(x_ref.shape[0] // ROW_SUB):
        rows = slice(sub * ROW_SUB, (sub + 1) * ROW_SUB)
        x = x_ref[rows, :]
        hb = _rms(x, pre_ref[...]).astype(BF16)
        for idx, (c0, cw) in enumerate(FFN_CHUNKS):
            g = jnp.dot(hb, wg_ref[:, c0:c0 + cw], preferred_element_type=F32)
            u = jnp.dot(hb, wu_ref[:, c0:c0 + cw], preferred_element_type=F32)
            a = (jax.nn.silu(g) * u).astype(BF16)
            part = jnp.dot(a, wd_ref[c0:c0 + cw, :], preferred_element_type=F32)
            if idx == 0:
                acc_ref[rows, :] = part
            else:
                acc_ref[rows, :] += part
        o_ref[rows, :] = x + FFN_RES_WEIGHT * _rms(acc_ref[rows, :], post_ref[...])


def _ffn(x, layer, pre_g, post_g, wg, wu, wd):
    t = x.shape[0]
    tm = FFN_TM
    row = pl.BlockSpec((tm, D_MODEL), lambda i: (i, 0))
    return pl.pallas_call(
        _ffn_kernel,
        grid=(t // tm,),
        in_specs=[row, _layer_spec((1, D_MODEL), layer), _layer_spec((1, D_MODEL), layer),
                  _layer_spec((D_MODEL, D_FF), layer), _layer_spec((D_MODEL, D_FF), layer),
                  _layer_spec((D_FF, D_MODEL), layer)],
        out_specs=row,
        out_shape=jax.ShapeDtypeStruct((t, D_MODEL), F32),
        scratch_shapes=[pltpu.VMEM((tm, D_MODEL), F32)],
        compiler_params=_params(1),
        name="ffn",
    )(x, pre_g, post_g, wg, wu, wd)


def _rope(x, cos, sin_lo, sin_hi):
    half = QK_ROPE_DIM // 2
    return x * cos + pltpu.roll(x, SLOT - half, 1) * sin_lo + pltpu.roll(x, half, 1) * sin_hi


def _proj_kernel(x_ref, g_ref, w1_ref, qg_ref, kvg_ref, wq_ref, wkv_ref, cos_ref, sin_lo_ref, sin_hi_ref,
                 qa_ref, ka_ref, va_ref, q_ref, k_ref, v_ref):
    hs = N_HEADS_B * SLOT
    for sub in range(x_ref.shape[0] // ROW_SUB):
        rows = slice(sub * ROW_SUB, (sub + 1) * ROW_SUB)
        hb = _rms(x_ref[rows, :], g_ref[...]).astype(BF16)
        proj = jnp.dot(hb, w1_ref[...], preferred_element_type=F32)
        qa_ref[rows, :] = proj[:, W1_QA[0]:W1_QA[1]].astype(BF16)
        ka_ref[rows, :] = proj[:, W1_KA[0]:W1_KA[1]].astype(BF16)
        va_ref[rows, :] = proj[:, W1_VA[0]:W1_VA[1]].astype(BF16)
        cqn = _rms(proj[:, W1_CQ[0]:W1_CQ[1]], qg_ref[...]).astype(BF16)
        ckvn = _rms(proj[:, W1_CKV[0]:W1_CKV[1]], kvg_ref[...]).astype(BF16)
        cos = cos_ref[rows, :]
        sin_lo = sin_lo_ref[rows, :]
        sin_hi = sin_hi_ref[rows, :]
        k_rope = _rope(proj[:, W1_KR[0]:W1_KR[1]], cos, sin_lo, sin_hi)
        qf = jnp.dot(cqn, wq_ref[...], preferred_element_type=F32)
        kvf = jnp.dot(ckvn, wkv_ref[...], preferred_element_type=F32)
        for h in range(N_HEADS_B):
            lo, hi = h * SLOT, (h + 1) * SLOT
            q_ref[rows, lo:hi] = _rope(qf[:, lo:hi], cos, sin_lo, sin_hi).astype(BF16)
            k_ref[rows, lo:hi] = (kvf[:, lo:hi] + k_rope).astype(BF16)
        v = kvf[:, hs:]
        pair_lane = lax.broadcasted_iota(jnp.int32, v.shape, 1) % (2 * SLOT)
        v_ref[rows, :] = jnp.where((pair_lane == V_DIM_B) | (pair_lane == SLOT), 1.0, v).astype(BF16)


def _proj(x, layer, g, w1, qg, kvg, wq, wkv, rope_tabs, seq):
    t = x.shape[0]
    tm = PROJ_TM
    nseq = seq // tm

    def row(cols):
        return pl.BlockSpec((tm, cols), lambda i: (i, 0))

    tab = pl.BlockSpec((tm, SLOT), lambda i: (i % nseq, 0))
    hs = N_HEADS_B * SLOT
    out_cols = (A_Q_COLS, W1_KA[1] - W1_KA[0], W1_VA[1] - W1_VA[0], hs, hs, hs)
    return pl.pallas_call(
        _proj_kernel,
        grid=(t // tm,),
        in_specs=[row(D_MODEL), _layer_spec((1, D_MODEL), layer), _const_spec((D_MODEL, W1_COLS)),
                  _layer_spec((1, Q_LORA_RANK), layer), _layer_spec((1, KV_LORA_RANK), layer),
                  _const_spec((Q_LORA_RANK, hs)), _const_spec((KV_LORA_RANK, 2 * hs)),
                  tab, tab, tab],
        out_specs=[row(c) for c in out_cols],
        out_shape=[jax.ShapeDtypeStruct((t, c), BF16) for c in out_cols],
        compiler_params=_params(1),
        name="proj",
    )(x, g, w1, qg, kvg, wq, wkv, *rope_tabs)


def _bias_kernel(rb_ref, bucket_ref, o_ref):
    bucket = bucket_ref[0]
    for h in range(N_HEADS_A):
        acc = jnp.full(bucket.shape, -jnp.inf, F32)
        for b in range(N_BUCKETS):
            acc = jnp.where(bucket == b, rb_ref[b, h], acc)
        o_ref[0, h] = acc


def _bias_table(rel_bias, bucket):
    kw = 3 * BLOCK
    return pl.pallas_call(
        _bias_kernel,
        grid=(3,),
        in_specs=[pl.BlockSpec(memory_space=pltpu.SMEM),
                  pl.BlockSpec((1, BLOCK, kw), lambda e: (e, 0, 0))],
        out_specs=pl.BlockSpec((1, N_HEADS_A, BLOCK, kw), lambda e: (e, 0, 0, 0)),
        out_shape=jax.ShapeDtypeStruct((3, N_HEADS_A, BLOCK, kw), F32),
        compiler_params=_params(1),
        name="t5_bias",
    )(rel_bias, bucket)


def _t5_bucket(rel):
    half = N_BUCKETS // 2
    max_exact = half // 2
    ret = jnp.where(rel > 0, half, 0)
    n = jnp.abs(rel)
    nf = jnp.maximum(n, 1).astype(F32)
    large = max_exact + (jnp.log(nf / max_exact) / math.log(MAX_DISTANCE / max_exact)
                         * (half - max_exact)).astype(jnp.int32)
    large = jnp.minimum(large, half - 1)
    return ret + jnp.where(n < max_exact, n, large)


def _band_buckets():
    qi = jnp.arange(BLOCK, dtype=jnp.int32)[:, None]
    si = jnp.arange(3 * BLOCK, dtype=jnp.int32)[None, :]
    out = []
    for shift in (0, -BLOCK, -2 * BLOCK):
        rel = si + shift - qi
        out.append(jnp.where(jnp.abs(rel) <= WINDOW, _t5_bucket(rel), -1))
    return jnp.stack(out).astype(jnp.int32)


def _win_kernel(q_ref, k_ref, v_ref, bias_ref, sink_ref, o_ref, *, seq):
    step = pl.program_id(1)
    nb = seq // BLOCK
    kw = 3 * BLOCK
    gw = GROUP_A * HEAD_DIM_A
    lane = lax.broadcasted_iota(jnp.int32, (BLOCK, gw), 1)
    keep = [(lane >= r * HEAD_DIM_A) & (lane < (r + 1) * HEAD_DIM_A) for r in range(GROUP_A)]
    for blk in range(WIN_BLOCKS):
        n = step * WIN_BLOCKS + blk
        start = pl.multiple_of(jnp.clip((n - 1) * BLOCK, 0, seq - kw), BLOCK)
        edge = jnp.where(n == 0, 0, jnp.where(n == nb - 1, 2, 1))
        rows = slice(blk * BLOCK, (blk + 1) * BLOCK)
        for g in range(N_KV_HEADS_A):
            q_grp = q_ref[rows, g * gw:(g + 1) * gw]
            q_stack = jnp.concatenate(
                [jnp.where(keep[r], q_grp, jnp.zeros_like(q_grp)) for r in range(GROUP_A)], axis=0)
            k_dup = k_ref[pl.ds(start, kw), g * SLOT:(g + 1) * SLOT]
            v_dup = v_ref[pl.ds(start, kw), g * SLOT:(g + 1) * SLOT]
            k_rep = jnp.concatenate([k_dup, k_dup], axis=1)
            v_rep = jnp.concatenate([v_dup, v_dup], axis=1)
            s = lax.dot_general(q_stack, k_rep, (((1,), (1,)), ((), ())),
                                preferred_element_type=F32)
            s = jnp.concatenate([s + bias_ref[edge, g], sink_ref[g]], axis=1)
            p = jnp.exp(s - jnp.max(s, axis=-1, keepdims=True))
            denom = jnp.sum(p, axis=-1, keepdims=True)
            o_all = jnp.dot(p[:, :kw].astype(BF16), v_rep, preferred_element_type=F32) / denom
            out = jnp.zeros((BLOCK, gw), F32)
            for r in range(GROUP_A):
                out = jnp.where(keep[r], o_all[r * BLOCK:(r + 1) * BLOCK], out)
            o_ref[rows, g * gw:(g + 1) * gw] = out.astype(BF16)


def _window_attention(qa, ka, va, bias, sink_col, batch, seq):
    tq = WIN_BLOCKS * BLOCK
    kc = ka.shape[-1]
    vc = va.shape[-1]
    return pl.pallas_call(
        functools.partial(_win_kernel, seq=seq),
        grid=(batch, seq // tq),
        in_specs=[pl.BlockSpec((None, tq, A_Q_COLS), lambda b, i: (b, i, 0)),
                  pl.BlockSpec((None, seq, kc), lambda b, i: (b, 0, 0)),
                  pl.BlockSpec((None, seq, vc), lambda b, i: (b, 0, 0)),
                  _const_spec(bias.shape), _const_spec(sink_col.shape)],
        out_specs=pl.BlockSpec((None, tq, A_Q_COLS), lambda b, i: (b, i, 0)),
        out_shape=jax.ShapeDtypeStruct((batch, seq, A_Q_COLS), BF16),
        compiler_params=_params(2),
        name="window_attn",
    )(qa.reshape(batch, seq, -1), ka.reshape(batch, seq, kc), va.reshape(batch, seq, vc), bias, sink_col)


def _mla_kernel(q_ref, k_ref, v_ref, o_ref, s0_ref, s1_ref, *, seq):
    c2 = (QK_NOPE_DIM + QK_ROPE_DIM) ** -0.5 * math.log2(math.e)
    n_chunks = seq // MLA_KC
    tq = q_ref.shape[0]
    lane = lax.broadcasted_iota(jnp.int32, (tq, SLOT), 1)
    s_bufs = (s0_ref, s1_ref)
    row0 = pl.multiple_of(jnp.minimum(pl.program_id(1), 0), 16)
    tile_rows = pl.ds(row0, tq)

    def scores(h):
        lo, hi = h * SLOT, (h + 1) * SLOT
        q_h = q_ref[:, lo:hi]
        m = jnp.full((tq, 1), -jnp.inf, F32)
        for c in range(n_chunks):
            rows = slice(c * MLA_KC, (c + 1) * MLA_KC)
            s = lax.dot_general(q_h, k_ref[rows, lo:hi], (((1,), (1,)), ((), ())),
                                preferred_element_type=F32)
            s_bufs[h % 2][tile_rows, rows] = s
            m = jnp.maximum(m, jnp.max(s, axis=-1, keepdims=True))
        return m

    def weighted(h, m):
        lo = (h // 2) * 2 * SLOT
        acc = jnp.zeros((tq, 2 * SLOT), F32)
        for c in range(n_chunks):
            rows = slice(c * MLA_KC, (c + 1) * MLA_KC)
            p = jnp.exp2((s_bufs[h % 2][tile_rows, rows] - m) * c2)
            acc = acc + jnp.dot(p.astype(BF16), v_ref[rows, lo:lo + 2 * SLOT],
                                preferred_element_type=F32)
        return acc[:, (h % 2) * SLOT:(h % 2 + 1) * SLOT]

    m = scores(0)
    even = None
    for h in range(N_HEADS_B):
        m_next = scores(h + 1) if h + 1 < N_HEADS_B else None
        acc = weighted(h, m)
        if h % 2 == 0:
            even = acc
        else:
            denom_even = even[:, V_DIM_B:V_DIM_B + 1]
            denom_odd = acc[:, 0:1]
            pair = jnp.where(lane < V_DIM_B, even / denom_even, acc / denom_odd)
            o_ref[:, (h // 2) * SLOT:(h // 2 + 1) * SLOT] = pair.astype(BF16)
        m = m_next


def _mla_attention(q, k, v, batch, seq):
    tq = MLA_TQ
    hs = N_HEADS_B * SLOT
    oc = N_HEADS_B * V_DIM_B
    return pl.pallas_call(
        functools.partial(_mla_kernel, seq=seq),
        grid=(batch, seq // tq),
        in_specs=[pl.BlockSpec((None, tq, hs), lambda b, i: (b, i, 0)),
                  pl.BlockSpec((None, seq, hs), lambda b, i: (b, 0, 0)),
                  pl.BlockSpec((None, seq, hs), lambda b, i: (b, 0, 0))],
        out_specs=pl.BlockSpec((None, tq, oc), lambda b, i: (b, i, 0)),
        out_shape=jax.ShapeDtypeStruct((batch, seq, oc), BF16),
        scratch_shapes=[pltpu.VMEM((tq, seq), F32), pltpu.VMEM((tq, seq), F32)],
        compiler_params=_params(2),
        name="mla_attn",
    )(q.reshape(batch, seq, hs), k.reshape(batch, seq, hs), v.reshape(batch, seq, hs))


def _merge_kernel(x_ref, oa_ref, ob_ref, pre_ref, post_ref, wga_ref, wgb_ref, wa_ref, wb_ref,
                  wo_ref, o_ref):
    for sub in range(x_ref.shape[0] // ROW_SUB):
        rows = slice(sub * ROW_SUB, (sub + 1) * ROW_SUB)
        x = x_ref[rows, :]
        hb = _rms(x, pre_ref[...]).astype(BF16)
        ga = jax.nn.sigmoid(jnp.dot(hb, wga_ref[...], preferred_element_type=F32))
        merged = ga * jnp.dot(oa_ref[rows, :], wa_ref[...], preferred_element_type=F32)
        gb = jax.nn.sigmoid(jnp.dot(hb, wgb_ref[...], preferred_element_type=F32))
        merged = merged + gb * jnp.dot(ob_ref[rows, :], wb_ref[...], preferred_element_type=F32)
        y = jnp.dot(merged.astype(BF16), wo_ref[...], preferred_element_type=F32)
        o_ref[rows, :] = x + _rms(y, post_ref[...])


def _merge(x, oa, ob, layer, pre_g, post_g, wga, wgb, wa, wb, wo):
    t = x.shape[0]
    tm = MERGE_TM

    def row(cols):
        return pl.BlockSpec((tm, cols), lambda i: (i, 0))

    bcols = N_HEADS_B * V_DIM_B
    return pl.pallas_call(
        _merge_kernel,
        grid=(t // tm,),
        in_specs=[row(D_MODEL), row(A_Q_COLS), row(bcols),
                  _layer_spec((1, D_MODEL), layer), _layer_spec((1, D_MODEL), layer),
                  _const_spec((D_MODEL, D_MODEL)), _const_spec((D_MODEL, D_MODEL)),
                  _layer_spec((A_Q_COLS, D_MODEL), layer), _layer_spec((bcols, D_MODEL), layer),
                  _layer_spec((D_MODEL, D_MODEL), layer)],
        out_specs=row(D_MODEL),
        out_shape=jax.ShapeDtypeStruct((t, D_MODEL), F32),
        compiler_params=_params(1),
        name="merge",
    )(x, oa.reshape(t, -1), ob.reshape(t, -1), pre_g, post_g, wga, wgb, wa, wb, wo)


def _layer_weights(wi, uq, ukv):
    o = 0
    pieces = {}
    for name, sz in (("qa", A_Q_COLS), ("ka", A_KV_COLS), ("va", A_KV_COLS), ("cq", Q_LORA_RANK),
                     ("ckv", KV_LORA_RANK), ("kr", QK_ROPE_DIM), ("ga", D_MODEL), ("gb", D_MODEL)):
        pieces[name] = wi[:, o:o + sz]
        o += sz
    z64 = jnp.zeros((D_MODEL, 64), BF16)
    z32 = jnp.zeros((D_MODEL, 32), BF16)
    ka = pieces["ka"]
    va = pieces["va"]
    k0, k1 = ka[:, :HEAD_DIM_A], ka[:, HEAD_DIM_A:]
    v0, v1 = va[:, :HEAD_DIM_A], va[:, HEAD_DIM_A:]
    w1 = jnp.concatenate([
        pieces["qa"] * (HEAD_DIM_A ** -0.5),
        k0, k0, k1, k1,
        v0, v0, v1, v1,
        pieces["cq"], pieces["ckv"],
        z64, pieces["kr"], z32,
    ], axis=1)

    uq = uq.reshape(Q_LORA_RANK, N_HEADS_B, QK_NOPE_DIM + QK_ROPE_DIM)
    zq = jnp.zeros((Q_LORA_RANK, N_HEADS_B, SLOT - QK_NOPE_DIM - QK_ROPE_DIM), BF16)
    wq = jnp.concatenate([uq, zq], axis=-1).reshape(Q_LORA_RANK, -1)

    ukv = ukv.reshape(KV_LORA_RANK, N_HEADS_B, QK_NOPE_DIM + V_DIM_B)
    k_nope, v_b = ukv[..., :QK_NOPE_DIM], ukv[..., QK_NOPE_DIM:]
    zk = jnp.zeros((KV_LORA_RANK, N_HEADS_B, SLOT - QK_NOPE_DIM), BF16)
    k_slots = jnp.concatenate([k_nope, zk], axis=-1).reshape(KV_LORA_RANK, -1)
    zv = jnp.zeros((KV_LORA_RANK, N_HEADS_B // 2, V_DIM_B), BF16)
    v_even = jnp.concatenate([v_b[:, 0::2], zv], axis=-1)
    v_odd = jnp.concatenate([zv, v_b[:, 1::2]], axis=-1)
    v_slots = jnp.stack([v_even, v_odd], axis=2).reshape(KV_LORA_RANK, -1)
    wkv = jnp.concatenate([k_slots, v_slots], axis=1)
    return w1, wq, wkv, pieces["ga"], pieces["gb"]


def _sink_block(sink_l):
    col = jnp.repeat(sink_l, BLOCK).reshape(N_KV_HEADS_A, GROUP_A * BLOCK, 1)
    lane = jnp.arange(SLOT)[None, None, :]
    return jnp.where(lane == 0, col, -jnp.inf).astype(F32)


def _rope_slot_tables(seq):
    half = QK_ROPE_DIM // 2
    inv_freq = ROPE_THETA ** (-jnp.arange(0, QK_ROPE_DIM, 2, dtype=F32) / QK_ROPE_DIM)
    ang = jnp.arange(seq, dtype=F32)[:, None] * inv_freq[None, :]
    cos, sin = jnp.cos(ang), jnp.sin(ang)
    z_half = jnp.zeros((seq, half), F32)
    z_nope = jnp.zeros((seq, QK_NOPE_DIM), F32)
    pad = jnp.zeros((seq, SLOT - QK_NOPE_DIM - QK_ROPE_DIM), F32)
    cos_tab = jnp.concatenate([jnp.ones((seq, QK_NOPE_DIM), F32), cos, cos, pad], axis=1)
    sin_lo = jnp.concatenate([z_nope, -sin, z_half, pad], axis=1)
    sin_hi = jnp.concatenate([z_nope, z_half, sin, pad], axis=1)
    return cos_tab, sin_lo, sin_hi


def _trunk(x3, bias, shared, layers):
    batch, seq, _ = x3.shape
    x = x3.reshape(batch * seq, D_MODEL)
    rope_tabs = _rope_slot_tables(seq)
    for l, lw in enumerate(layers):
        x = _ffn(x, l, *shared["ffn1"])
        qa, ka, va, q, k, v = _proj(x, l, shared["mix_pre_g"], lw["w1"], shared["q_norm_g"],
                                    shared["kv_norm_g"], lw["wq"], lw["wkv"], rope_tabs, seq)
        oa = _window_attention(qa, ka, va, bias, lw["sink_col"], batch, seq)
        ob = _mla_attention(q, k, v, batch, seq)
        x = _merge(x, oa, ob, l, shared["mix_pre_g"], shared["mix_post_g"], lw["wga"], lw["wgb"],
                   shared["w_a_out"], shared["w_b_out"], shared["w_o"])
        x = _ffn(x, l, *shared["ffn2"])
    return x.reshape(batch, seq, D_MODEL)


def kernel(x_prompt, x_sample, rel_bias, ffn1_pre_g, ffn1_post_g, ffn1_w_gate, ffn1_w_up, ffn1_w_down, mix_pre_g, mix_post_g, w_in, sink, q_norm_g, kv_norm_g, w_uq, w_ukv, w_a_out, w_b_out, w_o, ffn2_pre_g, ffn2_post_g, ffn2_w_gate, ffn2_w_up, ffn2_w_down):
    def gains(g):
        return g.reshape(DEPTH, 1, -1)

    def bf16(w):
        return w.astype(BF16)

    shared = dict(
        ffn1=(gains(ffn1_pre_g), gains(ffn1_post_g), bf16(ffn1_w_gate), bf16(ffn1_w_up), bf16(ffn1_w_down)),
        ffn2=(gains(ffn2_pre_g), gains(ffn2_post_g), bf16(ffn2_w_gate), bf16(ffn2_w_up), bf16(ffn2_w_down)),
        mix_pre_g=gains(mix_pre_g), mix_post_g=gains(mix_post_g),
        q_norm_g=gains(q_norm_g), kv_norm_g=gains(kv_norm_g),
        w_a_out=bf16(w_a_out), w_b_out=bf16(w_b_out), w_o=bf16(w_o),
    )
    w_in_b, w_uq_b, w_ukv_b = bf16(w_in), bf16(w_uq), bf16(w_ukv)
    layers = []
    for l in range(DEPTH):
        w1, wq, wkv, wga, wgb = _layer_weights(w_in_b[l], w_uq_b[l], w_ukv_b[l])
        layers.append(dict(w1=w1, wq=wq, wkv=wkv, wga=wga, wgb=wgb, sink_col=_sink_block(sink[l])))
    bias = _bias_table(rel_bias, _band_buckets())
    bias = bias.reshape(3, N_KV_HEADS_A, GROUP_A * BLOCK, 3 * BLOCK)
    return _trunk(x_prompt, bias, shared, layers), _trunk(x_sample, bias, shared, layers)
```

```python
import functools
import math

import jax
import jax.numpy as jnp
from jax import lax
from jax.experimental import pallas as pl
from jax.experimental.pallas import tpu as pltpu

F32 = jnp.float32
BF16 = jnp.bfloat16

D_MODEL = 1024
DEPTH = 2
N_HEADS_A = 8
N_KV_HEADS_A = 2
GROUP_A = N_HEADS_A // N_KV_HEADS_A
HEAD_DIM_A = 64
WINDOW = 128
BLOCK = 128
N_HEADS_B = 8
QK_NOPE_DIM = 64
QK_ROPE_DIM = 32
V_DIM_B = 64
Q_LORA_RANK = 384
KV_LORA_RANK = 256
ROPE_THETA = 10000.0
N_BUCKETS = 32
MAX_DISTANCE = 128
D_FF = 2816
FFN_RES_WEIGHT = 0.5
EPS = 1e-6

A_Q_COLS = N_HEADS_A * HEAD_DIM_A
A_KV_COLS = N_KV_HEADS_A * HEAD_DIM_A

LANES = 128
SLOT = LANES
VMEM_LIMIT_BYTES = 56 * 1024 * 1024

W1_QA = (0, 512)
W1_KA = (512, 768)
W1_VA = (768, 1024)
W1_CQ = (1024, 1408)
W1_CKV = (1408, 1664)
W1_KR = (1664, 1792)
W1_COLS = 1792

FFN_CHUNKS = ((0, 1024), (1024, 1024), (2048, 768))
ROW_SUB = 256
FFN_TM = 1024
PROJ_TM = 1024
MERGE_TM = 1024
MLA_TILE_SCORES = 256 * 4096
MLA_SUB = 256
MLA_KC = 1024
WIN_BLOCKS = 8


def _rms(x, g):
    return x * lax.rsqrt(jnp.mean(x * x, axis=-1, keepdims=True) + EPS) * g


def _const_spec(shape):
    nd = len(shape)
    return pl.BlockSpec(shape, lambda *_: (0,) * nd, pipeline_mode=pl.Buffered(1))


def _layer_spec(shape, layer):
    nd = len(shape)
    return pl.BlockSpec((None,) + tuple(shape), lambda *_: (layer,) + (0,) * nd,
                        pipeline_mode=pl.Buffered(1))


def _params(n_axes, flags=None):
    return pltpu.CompilerParams(
        dimension_semantics=("parallel",) * n_axes,
        vmem_limit_bytes=VMEM_LIMIT_BYTES,
        flags=flags,
    )


def _ffn_kernel(x_ref, pre_ref, post_ref, wg_ref, wu_ref, wd_ref, o_ref, acc_ref):
    for sub in range(x_ref.shape[0] // ROW_SUB):
        rows = slice(sub * ROW_SUB, (sub + 1) * ROW_SUB)
        x = x_ref[rows, :]
        hb = _rms(x, pre_ref[...]).astype(BF16)
        for idx, (c0, cw) in enumerate(FFN_CHUNKS):
            g = jnp.dot(hb, wg_ref[:, c0:c0 + cw], preferred_element_type=F32)
            u = jnp.dot(hb, wu_ref[:, c0:c0 + cw], preferred_element_type=F32)
            a = (jax.nn.silu(g) * u).astype(BF16)
            part = jnp.dot(a, wd_ref[c0:c0 + cw, :], preferred_element_type=F32)
            if idx == 0:
                acc_ref[rows, :] = part
            else:
                acc_ref[rows, :] += part
        o_ref[rows, :] = x + FFN_RES_WEIGHT * _rms(acc_ref[rows, :], post_ref[...])


def _ffn(x, layer, pre_g, post_g, wg, wu, wd):
    t = x.shape[0]
    tm = FFN_TM
    row = pl.BlockSpec((tm, D_MODEL), lambda i: (i, 0))
    return pl.pallas_call(
        _ffn_kernel,
        grid=(t // tm,),
        in_specs=[row, _layer_spec((1, D_MODEL), layer), _layer_spec((1, D_MODEL), layer),
                  _layer_spec((D_MODEL, D_FF), layer), _layer_spec((D_MODEL, D_FF), layer),
                  _layer_spec((D_FF, D_MODEL), layer)],
        out_specs=row,
        out_shape=jax.ShapeDtypeStruct((t, D_MODEL), F32),
        scratch_shapes=[pltpu.VMEM((tm, D_MODEL), F32)],
        compiler_params=_params(1),
        name="ffn",
    )(x, pre_g, post_g, wg, wu, wd)


def _rope(x, cos, sin_lo, sin_hi):
    half = QK_ROPE_DIM // 2
    return x * cos + pltpu.roll(x, SLOT - half, 1) * sin_lo + pltpu.roll(x, half, 1) * sin_hi


def _proj_kernel(x_ref, g_ref, w1_ref, qg_ref, kvg_ref, wq_ref, wkv_ref, cos_ref, sin_lo_ref, sin_hi_ref,
                 qa_ref, ka_ref, va_ref, q_ref, k_ref, v_ref):
    hs = N_HEADS_B * SLOT
    for sub in range(x_ref.shape[0] // ROW_SUB):
        rows = slice(sub * ROW_SUB, (sub + 1) * ROW_SUB)
        hb = _rms(x_ref[rows, :], g_ref[...]).astype(BF16)
        proj = jnp.dot(hb, w1_ref[...], preferred_element_type=F32)
        qa_ref[rows, :] = proj[:, W1_QA[0]:W1_QA[1]].astype(BF16)
        ka_ref[rows, :] = proj[:, W1_KA[0]:W1_KA[1]].astype(BF16)
        va_ref[rows, :] = proj[:, W1_VA[0]:W1_VA[1]].astype(BF16)
        cqn = _rms(proj[:, W1_CQ[0]:W1_CQ[1]], qg_ref[...]).astype(BF16)
        ckvn = _rms(proj[:, W1_CKV[0]:W1_CKV[1]], kvg_ref[...]).astype(BF16)
        cos = cos_ref[rows, :]
        sin_lo = sin_lo_ref[rows, :]
        sin_hi = sin_hi_ref[rows, :]
        k_rope = _rope(proj[:, W1_KR[0]:W1_KR[1]], cos, sin_lo, sin_hi)
        qf = jnp.dot(cqn, wq_ref[...], preferred_element_type=F32)
        kvf = jnp.dot(ckvn, wkv_ref[...], preferred_element_type=F32)
        for h in range(N_HEADS_B):
            lo, hi = h * SLOT, (h + 1) * SLOT
            q_ref[rows, lo:hi] = _rope(qf[:, lo:hi], cos, sin_lo, sin_hi).astype(BF16)
            k_ref[rows, lo:hi] = (kvf[:, lo:hi] + k_rope).astype(BF16)
        v = kvf[:, hs:]
        pair_lane = lax.broadcasted_iota(jnp.int32, v.shape, 1) % (2 * SLOT)
        v_ref[rows, :] = jnp.where((pair_lane == V_DIM_B) | (pair_lane == SLOT), 1.0, v).astype(BF16)


def _proj(x, layer, g, w1, qg, kvg, wq, wkv, rope_tabs, seq):
    t = x.shape[0]
    tm = PROJ_TM
    nseq = seq // tm

    def row(cols):
        return pl.BlockSpec((tm, cols), lambda i: (i, 0))

    tab = pl.BlockSpec((tm, SLOT), lambda i: (i % nseq, 0))
    hs = N_HEADS_B * SLOT
    out_cols = (A_Q_COLS, W1_KA[1] - W1_KA[0], W1_VA[1] - W1_VA[0], hs, hs, hs)
    return pl.pallas_call(
        _proj_kernel,
        grid=(t // tm,),
        in_specs=[row(D_MODEL), _layer_spec((1, D_MODEL), layer), _const_spec((D_MODEL, W1_COLS)),
                  _layer_spec((1, Q_LORA_RANK), layer), _layer_spec((1, KV_LORA_RANK), layer),
                  _const_spec((Q_LORA_RANK, hs)), _const_spec((KV_LORA_RANK, 2 * hs)),
                  tab, tab, tab],
        out_specs=[row(c) for c in out_cols],
        out_shape=[jax.ShapeDtypeStruct((t, c), BF16) for c in out_cols],
        compiler_params=_params(1),
        name="proj",
    )(x, g, w1, qg, kvg, wq, wkv, *rope_tabs)


def _bias_kernel(rb_ref, bucket_ref, o_ref):
    bucket = bucket_ref[0]
    for h in range(N_HEADS_A):
        acc = jnp.full(bucket.shape, -jnp.inf, F32)
        for b in range(N_BUCKETS):
            acc = jnp.where(bucket == b, rb_ref[b, h], acc)
        o_ref[0, h] = acc


def _bias_table(rel_bias, bucket):
    kw = 3 * BLOCK
    return pl.pallas_call(
        _bias_kernel,
        grid=(3,),
        in_specs=[pl.BlockSpec(memory_space=pltpu.SMEM),
                  pl.BlockSpec((1, BLOCK, kw), lambda e: (e, 0, 0))],
        out_specs=pl.BlockSpec((1, N_HEADS_A, BLOCK, kw), lambda e: (e, 0, 0, 0)),
        out_shape=jax.ShapeDtypeStruct((3, N_HEADS_A, BLOCK, kw), F32),
        compiler_params=_params(1),
        name="t5_bias",
    )(rel_bias, bucket)


def _t5_bucket(rel):
    half = N_BUCKETS // 2
    max_exact = half // 2
    ret = jnp.where(rel > 0, half, 0)
    n = jnp.abs(rel)
    nf = jnp.maximum(n, 1).astype(F32)
    large = max_exact + (jnp.log(nf / max_exact) / math.log(MAX_DISTANCE / max_exact)
                         * (half - max_exact)).astype(jnp.int32)
    large = jnp.minimum(large, half - 1)
    return ret + jnp.where(n < max_exact, n, large)


def _band_buckets():
    qi = jnp.arange(BLOCK, dtype=jnp.int32)[:, None]
    si = jnp.arange(3 * BLOCK, dtype=jnp.int32)[None, :]
    out = []
    for shift in (0, -BLOCK, -2 * BLOCK):
        rel = si + shift - qi
        out.append(jnp.where(jnp.abs(rel) <= WINDOW, _t5_bucket(rel), -1))
    return jnp.stack(out).astype(jnp.int32)


def _win_kernel(q_ref, k_ref, v_ref, bias_ref, sink_ref, o_ref, *, seq):
    step = pl.program_id(1)
    nb = seq // BLOCK
    kw = 3 * BLOCK
    gw = GROUP_A * HEAD_DIM_A
    lane = lax.broadcasted_iota(jnp.int32, (BLOCK, gw), 1)
    keep = [(lane >= r * HEAD_DIM_A) & (lane < (r + 1) * HEAD_DIM_A) for r in range(GROUP_A)]
    for blk in range(WIN_BLOCKS):
        n = step * WIN_BLOCKS + blk
        start = pl.multiple_of(jnp.clip((n - 1) * BLOCK, 0, seq - kw), BLOCK)
        edge = jnp.where(n == 0, 0, jnp.where(n == nb - 1, 2, 1))
        rows = slice(blk * BLOCK, (blk + 1) * BLOCK)
        for g in range(N_KV_HEADS_A):
            q_grp = q_ref[rows, g * gw:(g + 1) * gw]
            q_stack = jnp.concatenate(
                [jnp.where(keep[r], q_grp, jnp.zeros_like(q_grp)) for r in range(GROUP_A)], axis=0)
            k_dup = k_ref[pl.ds(start, kw), g * SLOT:(g + 1) * SLOT]
            v_dup = v_ref[pl.ds(start, kw), g * SLOT:(g + 1) * SLOT]
            k_rep = jnp.concatenate([k_dup, k_dup], axis=1)
            v_rep = jnp.concatenate([v_dup, v_dup], axis=1)
            s = lax.dot_general(q_stack, k_rep, (((1,), (1,)), ((), ())),
                                preferred_element_type=F32)
            s = jnp.concatenate([s + bias_ref[edge, g], sink_ref[g]], axis=1)
            p = jnp.exp(s - jnp.max(s, axis=-1, keepdims=True))
            denom = jnp.sum(p, axis=-1, keepdims=True)
            o_all = jnp.dot(p[:, :kw].astype(BF16), v_rep, preferred_element_type=F32) / denom
            out = jnp.zeros((BLOCK, gw), F32)
            for r in range(GROUP_A):
                out = jnp.where(keep[r], o_all[r * BLOCK:(r + 1) * BLOCK], out)
            o_ref[rows, g * gw:(g + 1) * gw] = out.astype(BF16)


def _window_attention(qa, ka, va, bias, sink_rows, batch, seq):
    tq = WIN_BLOCKS * BLOCK
    kc = ka.shape[-1]
    vc = va.shape[-1]
    return pl.pallas_call(
        functools.partial(_win_kernel, seq=seq),
        grid=(batch, seq // tq),
        in_specs=[pl.BlockSpec((None, tq, A_Q_COLS), lambda b, i: (b, i, 0)),
                  pl.BlockSpec((None, seq, kc), lambda b, i: (b, 0, 0)),
                  pl.BlockSpec((None, seq, vc), lambda b, i: (b, 0, 0)),
                  _const_spec(bias.shape), _const_spec(sink_rows.shape)],
        out_specs=pl.BlockSpec((None, tq, A_Q_COLS), lambda b, i: (b, i, 0)),
        out_shape=jax.ShapeDtypeStruct((batch, seq, A_Q_COLS), BF16),
        compiler_params=_params(2),
        name="window_attn",
    )(qa.reshape(batch, seq, -1), ka.reshape(batch, seq, kc), va.reshape(batch, seq, vc), bias, sink_rows)


def _mla_kernel(q_ref, k_ref, v_ref, o_ref, s0_ref, s1_ref, *, seq):
    c2 = (QK_NOPE_DIM + QK_ROPE_DIM) ** -0.5 * math.log2(math.e)
    n_chunks = seq // MLA_KC
    sub = MLA_SUB
    lane = lax.broadcasted_iota(jnp.int32, (sub, SLOT), 1)
    s_bufs = (s0_ref, s1_ref)
    row0 = pl.multiple_of(jnp.minimum(pl.program_id(1), 0), 16)
    tile_rows = pl.ds(row0, sub)
    units = [(t, h) for t in range(q_ref.shape[0] // sub) for h in range(N_HEADS_B)]

    def scores(u):
        t, h = units[u]
        q_h = q_ref[t * sub:(t + 1) * sub, h * SLOT:(h + 1) * SLOT]
        m = jnp.full((sub, 1), -jnp.inf, F32)
        for c in range(n_chunks):
            rows = slice(c * MLA_KC, (c + 1) * MLA_KC)
            s = lax.dot_general(q_h, k_ref[rows, h * SLOT:(h + 1) * SLOT], (((1,), (1,)), ((), ())),
                                preferred_element_type=F32)
            s_bufs[u % 2][tile_rows, rows] = s
            m = jnp.maximum(m, jnp.max(s, axis=-1, keepdims=True))
        return m

    def weighted(u, m):
        _, h = units[u]
        lo = (h // 2) * 2 * SLOT
        acc = jnp.zeros((sub, 2 * SLOT), F32)
        for c in range(n_chunks):
            rows = slice(c * MLA_KC, (c + 1) * MLA_KC)
            p = jnp.exp2((s_bufs[u % 2][tile_rows, rows] - m) * c2)
            acc = acc + jnp.dot(p.astype(BF16), v_ref[rows, lo:lo + 2 * SLOT],
                                preferred_element_type=F32)
        return acc[:, (h % 2) * SLOT:(h % 2 + 1) * SLOT]

    m = scores(0)
    even = None
    for u, (t, h) in enumerate(units):
        m_next = scores(u + 1) if u + 1 < len(units) else None
        acc = weighted(u, m)
        if h % 2 == 0:
            even = acc
        else:
            denom_even = even[:, V_DIM_B:V_DIM_B + 1]
            denom_odd = acc[:, 0:1]
            pair = jnp.where(lane < V_DIM_B, even / denom_even, acc / denom_odd)
            o_ref[t * sub:(t + 1) * sub, (h // 2) * SLOT:(h // 2 + 1) * SLOT] = pair.astype(BF16)
        m = m_next


def _mla_attention(q, k, v, batch, seq):
    tq = max(MLA_SUB, MLA_TILE_SCORES // seq)
    hs = N_HEADS_B * SLOT
    oc = N_HEADS_B * V_DIM_B
    return pl.pallas_call(
        functools.partial(_mla_kernel, seq=seq),
        grid=(batch, seq // tq),
        in_specs=[pl.BlockSpec((None, tq, hs), lambda b, i: (b, i, 0)),
                  pl.BlockSpec((None, seq, hs), lambda b, i: (b, 0, 0)),
                  pl.BlockSpec((None, seq, hs), lambda b, i: (b, 0, 0))],
        out_specs=pl.BlockSpec((None, tq, oc), lambda b, i: (b, i, 0)),
        out_shape=jax.ShapeDtypeStruct((batch, seq, oc), BF16),
        scratch_shapes=[pltpu.VMEM((MLA_SUB, seq), F32), pltpu.VMEM((MLA_SUB, seq), F32)],
        compiler_params=_params(2),
        name="mla_attn",
    )(q.reshape(batch, seq, hs), k.reshape(batch, seq, hs), v.reshape(batch, seq, hs))


def _merge_kernel(x_ref, oa_ref, ob_ref, pre_ref, post_ref, wga_ref, wgb_ref, wa_ref, wb_ref,
                  wo_ref, o_ref):
    for sub in range(x_ref.shape[0] // ROW_SUB):
        rows = slice(sub * ROW_SUB, (sub + 1) * ROW_SUB)
        x = x_ref[rows, :]
        hb = _rms(x, pre_ref[...]).astype(BF16)
        ga = jax.nn.sigmoid(jnp.dot(hb, wga_ref[...], preferred_element_type=F32))
        merged = ga * jnp.dot(oa_ref[rows, :], wa_ref[...], preferred_element_type=F32)
        gb = jax.nn.sigmoid(jnp.dot(hb, wgb_ref[...], preferred_element_type=F32))
        merged = merged + gb * jnp.dot(ob_ref[rows, :], wb_ref[...], preferred_element_type=F32)
        y = jnp.dot(merged.astype(BF16), wo_ref[...], preferred_element_type=F32)
        o_ref[rows, :] = x + _rms(y, post_ref[...])


def _merge(x, oa, ob, layer, pre_g, post_g, wga, wgb, wa, wb, wo):
    t = x.shape[0]
    tm = MERGE_TM

    def row(cols):
        return pl.BlockSpec((tm, cols), lambda i: (i, 0))

    bcols = N_HEADS_B * V_DIM_B
    return pl.pallas_call(
        _merge_kernel,
        grid=(t // tm,),
        in_specs=[row(D_MODEL), row(A_Q_COLS), row(bcols),
                  _layer_spec((1, D_MODEL), layer), _layer_spec((1, D_MODEL), layer),
                  _const_spec((D_MODEL, D_MODEL)), _const_spec((D_MODEL, D_MODEL)),
                  _layer_spec((A_Q_COLS, D_MODEL), layer), _layer_spec((bcols, D_MODEL), layer),
                  _layer_spec((D_MODEL, D_MODEL), layer)],
        out_specs=row(D_MODEL),
        out_shape=jax.ShapeDtypeStruct((t, D_MODEL), F32),
        compiler_params=_params(1),
        name="merge",
    )(x, oa.reshape(t, -1), ob.reshape(t, -1), pre_g, post_g, wga, wgb, wa, wb, wo)


def _layer_weights(wi, uq, ukv):
    o = 0
    pieces = {}
    for name, sz in (("qa", A_Q_COLS), ("ka", A_KV_COLS), ("va", A_KV_COLS), ("cq", Q_LORA_RANK),
                     ("ckv", KV_LORA_RANK), ("kr", QK_ROPE_DIM), ("ga", D_MODEL), ("gb", D_MODEL)):
        pieces[name] = wi[:, o:o + sz]
        o += sz
    z64 = jnp.zeros((D_MODEL, 64), BF16)
    z32 = jnp.zeros((D_MODEL, 32), BF16)
    ka = pieces["ka"]
    va = pieces["va"]
    k0, k1 = ka[:, :HEAD_DIM_A], ka[:, HEAD_DIM_A:]
    v0, v1 = va[:, :HEAD_DIM_A], va[:, HEAD_DIM_A:]
    w1 = jnp.concatenate([
        pieces["qa"] * (HEAD_DIM_A ** -0.5),
        k0, k0, k1, k1,
        v0, v0, v1, v1,
        pieces["cq"], pieces["ckv"],
        z64, pieces["kr"], z32,
    ], axis=1)

    uq = uq.reshape(Q_LORA_RANK, N_HEADS_B, QK_NOPE_DIM + QK_ROPE_DIM)
    zq = jnp.zeros((Q_LORA_RANK, N_HEADS_B, SLOT - QK_NOPE_DIM - QK_ROPE_DIM), BF16)
    wq = jnp.concatenate([uq, zq], axis=-1).reshape(Q_LORA_RANK, -1)

    ukv = ukv.reshape(KV_LORA_RANK, N_HEADS_B, QK_NOPE_DIM + V_DIM_B)
    k_nope, v_b = ukv[..., :QK_NOPE_DIM], ukv[..., QK_NOPE_DIM:]
    zk = jnp.zeros((KV_LORA_RANK, N_HEADS_B, SLOT - QK_NOPE_DIM), BF16)
    k_slots = jnp.concatenate([k_nope, zk], axis=-1).reshape(KV_LORA_RANK, -1)
    zv = jnp.zeros((KV_LORA_RANK, N_HEADS_B // 2, V_DIM_B), BF16)
    v_even = jnp.concatenate([v_b[:, 0::2], zv], axis=-1)
    v_odd = jnp.concatenate([zv, v_b[:, 1::2]], axis=-1)
    v_slots = jnp.stack([v_even, v_odd], axis=2).reshape(KV_LORA_RANK, -1)
    wkv = jnp.concatenate([k_slots, v_slots], axis=1)
    return w1, wq, wkv, pieces["ga"], pieces["gb"]


def _sink_rows(sink_l):
    col = jnp.repeat(sink_l, BLOCK).reshape(N_KV_HEADS_A, GROUP_A * BLOCK, 1)
    lane = jnp.arange(SLOT)[None, None, :]
    return jnp.where(lane == 0, col, -jnp.inf).astype(F32)


def _rope_slot_tables(seq):
    half = QK_ROPE_DIM // 2
    inv_freq = ROPE_THETA ** (-jnp.arange(0, QK_ROPE_DIM, 2, dtype=F32) / QK_ROPE_DIM)
    ang = jnp.arange(seq, dtype=F32)[:, None] * inv_freq[None, :]
    cos, sin = jnp.cos(ang), jnp.sin(ang)
    z_half = jnp.zeros((seq, half), F32)
    z_nope = jnp.zeros((seq, QK_NOPE_DIM), F32)
    pad = jnp.zeros((seq, SLOT - QK_NOPE_DIM - QK_ROPE_DIM), F32)
    cos_tab = jnp.concatenate([jnp.ones((seq, QK_NOPE_DIM), F32), cos, cos, pad], axis=1)
    sin_lo = jnp.concatenate([z_nope, -sin, z_half, pad], axis=1)
    sin_hi = jnp.concatenate([z_nope, z_half, sin, pad], axis=1)
    return cos_tab, sin_lo, sin_hi


def _trunk(x3, bias, shared, layers):
    batch, seq, _ = x3.shape
    x = x3.reshape(batch * seq, D_MODEL)
    rope_tabs = _rope_slot_tables(seq)
    for l, lw in enumerate(layers):
        x = _ffn(x, l, *shared["ffn1"])
        qa, ka, va, q, k, v = _proj(x, l, shared["mix_pre_g"], lw["w1"], shared["q_norm_g"],
                                    shared["kv_norm_g"], lw["wq"], lw["wkv"], rope_tabs, seq)
        oa = _window_attention(qa, ka, va, bias, lw["sink_rows"], batch, seq)
        ob = _mla_attention(q, k, v, batch, seq)
        x = _merge(x, oa, ob, l, shared["mix_pre_g"], shared["mix_post_g"], lw["wga"], lw["wgb"],
                   shared["w_a_out"], shared["w_b_out"], shared["w_o"])
        x = _ffn(x, l, *shared["ffn2"])
    return x.reshape(batch, seq, D_MODEL)


def kernel(x_prompt, x_sample, rel_bias, ffn1_pre_g, ffn1_post_g, ffn1_w_gate, ffn1_w_up, ffn1_w_down, mix_pre_g, mix_post_g, w_in, sink, q_norm_g, kv_norm_g, w_uq, w_ukv, w_a_out, w_b_out, w_o, ffn2_pre_g, ffn2_post_g, ffn2_w_gate, ffn2_w_up, ffn2_w_down):
    def gains(g):
        return g.reshape(DEPTH, 1, -1)

    def bf16(w):
        return w.astype(BF16)

    shared = dict(
        ffn1=(gains(ffn1_pre_g), gains(ffn1_post_g), bf16(ffn1_w_gate), bf16(ffn1_w_up), bf16(ffn1_w_down)),
        ffn2=(gains(ffn2_pre_g), gains(ffn2_post_g), bf16(ffn2_w_gate), bf16(ffn2_w_up), bf16(ffn2_w_down)),
        mix_pre_g=gains(mix_pre_g), mix_post_g=gains(mix_post_g),
        q_norm_g=gains(q_norm_g), kv_norm_g=gains(kv_norm_g),
        w_a_out=bf16(w_a_out), w_b_out=bf16(w_b_out), w_o=bf16(w_o),
    )
    w_in_b, w_uq_b, w_ukv_b = bf16(w_in), bf16(w_uq), bf16(w_ukv)
    layers = []
    for l in range(DEPTH):
        w1, wq, wkv, wga, wgb = _layer_weights(w_in_b[l], w_uq_b[l], w_ukv_b[l])
        layers.append(dict(w1=w1, wq=wq, wkv=wkv, wga=wga, wgb=wgb, sink_rows=_sink_rows(sink[l])))
    bias = _bias_table(rel_bias, _band_buckets())
    bias = bias.reshape(3, N_KV_HEADS_A, GROUP_A * BLOCK, 3 * BLOCK)
    return _trunk(x_prompt, bias, shared, layers), _trunk(x_sample, bias, shared, layers)
```

```python
import functools
import math

import jax
import jax.numpy as jnp
from jax import lax
from jax.experimental import pallas as pl
from jax.experimental.pallas import tpu as pltpu

F32 = jnp.float32
BF16 = jnp.bfloat16

D_MODEL = 1024
DEPTH = 2
N_HEADS_A = 8
N_KV_HEADS_A = 2
GROUP_A = N_HEADS_A // N_KV_HEADS_A
HEAD_DIM_A = 64
WINDOW = 128
BLOCK = 128
N_HEADS_B = 8
QK_NOPE_DIM = 64
QK_ROPE_DIM = 32
V_DIM_B = 64
Q_LORA_RANK = 384
KV_LORA_RANK = 256
ROPE_THETA = 10000.0
N_BUCKETS = 32
MAX_DISTANCE = 128
D_FF = 2816
FFN_RES_WEIGHT = 0.5
EPS = 1e-6

A_Q_COLS = N_HEADS_A * HEAD_DIM_A
A_KV_COLS = N_KV_HEADS_A * HEAD_DIM_A

LANES = 128
SLOT = LANES
VMEM_LIMIT_BYTES = 56 * 1024 * 1024

W1_QA = (0, 512)
W1_KA = (512, 640)
W1_VA = (640, 768)
W1_CQ = (768, 1152)
W1_CKV = (1152, 1408)
W1_KR = (1408, 1536)
W1_COLS = 1536

FFN_CHUNKS = ((0, 1024), (1024, 1024), (2048, 768))
ROW_SUB = 256
FFN_TM = 1024
PROJ_TM = 1024
MERGE_TM = 1024
MLA_TQ = 512
MLA_SUB = 256
MLA_KC = 1024
WIN_BLOCKS = 8


def _rms(x, g):
    return x * lax.rsqrt(jnp.mean(x * x, axis=-1, keepdims=True) + EPS) * g


def _const_spec(shape):
    nd = len(shape)
    return pl.BlockSpec(shape, lambda *_: (0,) * nd, pipeline_mode=pl.Buffered(1))


def _layer_spec(shape, layer):
    nd = len(shape)
    return pl.BlockSpec((None,) + tuple(shape), lambda *_: (layer,) + (0,) * nd,
                        pipeline_mode=pl.Buffered(1))


def _params(n_axes, flags=None):
    return pltpu.CompilerParams(
        dimension_semantics=("parallel",) * n_axes,
        vmem_limit_bytes=VMEM_LIMIT_BYTES,
        flags=flags,
    )


def _stream_specs(tm, cols, split, n):
    if n == 1:
        return [pl.BlockSpec((tm, cols), lambda i: (i, 0))]
    return [pl.BlockSpec((tm, cols), lambda i: (jnp.minimum(i, split - 1), 0)),
            pl.BlockSpec((tm, cols), lambda i: (jnp.maximum(i - split, 0), 0))]


def _stream_rows(refs, rows, first):
    if len(refs) == 1:
        return refs[0][rows, :]
    return jnp.where(first, refs[0][rows, :], refs[1][rows, :])


def _ffn_kernel(*refs, n_in, n_out, split):
    x_refs, refs = refs[:n_in], refs[n_in:]
    pre_ref, post_ref, wg_ref, wu_ref, wd_ref = refs[:5]
    o_refs, acc_ref = refs[5:5 + n_out], refs[5 + n_out]
    first = pl.program_id(0) < split
    for sub in range(x_refs[0].shape[0] // ROW_SUB):
        rows = slice(sub * ROW_SUB, (sub + 1) * ROW_SUB)
        x = _stream_rows(x_refs, rows, first)
        hb = _rms(x, pre_ref[...]).astype(BF16)
        for idx, (c0, cw) in enumerate(FFN_CHUNKS):
            g = jnp.dot(hb, wg_ref[:, c0:c0 + cw], preferred_element_type=F32)
            u = jnp.dot(hb, wu_ref[:, c0:c0 + cw], preferred_element_type=F32)
            a = (jax.nn.silu(g) * u).astype(BF16)
            part = jnp.dot(a, wd_ref[c0:c0 + cw, :], preferred_element_type=F32)
            if idx == 0:
                acc_ref[rows, :] = part
            else:
                acc_ref[rows, :] += part
        y = x + FFN_RES_WEIGHT * _rms(acc_ref[rows, :], post_ref[...])
        o_refs[-1][rows, :] = y
    if n_out == 2:
        @pl.when(first)
        def _():
            o_refs[0][...] = o_refs[1][...]


def _ffn(xs, layer, pre_g, post_g, wg, wu, wd, split, n_out=1):
    tm = FFN_TM
    t = sum(x.shape[0] for x in xs)
    n_in = len(xs)
    if n_out == 1:
        out_shape = [jax.ShapeDtypeStruct((t, D_MODEL), F32)]
    else:
        out_shape = [jax.ShapeDtypeStruct((split * tm, D_MODEL), F32),
                     jax.ShapeDtypeStruct((t - split * tm, D_MODEL), F32)]
    outs = pl.pallas_call(
        functools.partial(_ffn_kernel, n_in=n_in, n_out=n_out, split=split),
        grid=(t // tm,),
        in_specs=_stream_specs(tm, D_MODEL, split, n_in)
        + [_layer_spec((1, D_MODEL), layer), _layer_spec((1, D_MODEL), layer),
           _layer_spec((D_MODEL, D_FF), layer), _layer_spec((D_MODEL, D_FF), layer),
           _layer_spec((D_FF, D_MODEL), layer)],
        out_specs=_stream_specs(tm, D_MODEL, split, n_out),
        out_shape=out_shape,
        scratch_shapes=[pltpu.VMEM((tm, D_MODEL), F32)],
        compiler_params=pltpu.CompilerParams(dimension_semantics=("arbitrary",),
                                             vmem_limit_bytes=VMEM_LIMIT_BYTES),
        name="ffn",
    )(*xs, pre_g, post_g, wg, wu, wd)
    return outs[0] if n_out == 1 else outs


def _rope(x, cos, sin_lo, sin_hi):
    half = QK_ROPE_DIM // 2
    return x * cos + pltpu.roll(x, SLOT - half, 1) * sin_lo + pltpu.roll(x, half, 1) * sin_hi


def _proj_kernel(x_ref, g_ref, w1_ref, qg_ref, kvg_ref, wq_ref, wkv_ref, cos_ref, sin_lo_ref, sin_hi_ref,
                 qa_ref, ka_ref, va_ref, q_ref, k_ref, v_ref):
    hs = N_HEADS_B * SLOT
    half = SLOT // 2
    lane = lax.broadcasted_iota(jnp.int32, (ROW_SUB, SLOT), 1)
    low = lane < half
    one_at_zero = jnp.where(lane == 0, 1.0, 0.0)

    def dup_halves(x):
        swapped = pltpu.roll(x, half, 1)
        return jnp.where(low, x, swapped), jnp.where(low, swapped, x)

    for sub in range(x_ref.shape[0] // ROW_SUB):
        rows = slice(sub * ROW_SUB, (sub + 1) * ROW_SUB)
        hb = _rms(x_ref[rows, :], g_ref[...]).astype(BF16)
        proj = jnp.dot(hb, w1_ref[...], preferred_element_type=F32)
        qa_ref[rows, :] = proj[:, W1_QA[0]:W1_QA[1]].astype(BF16)
        for src, dst in ((W1_KA, ka_ref), (W1_VA, va_ref)):
            first, second = dup_halves(proj[:, src[0]:src[1]])
            dst[rows, :SLOT] = first.astype(BF16)
            dst[rows, SLOT:] = second.astype(BF16)
        cqn = _rms(proj[:, W1_CQ[0]:W1_CQ[1]], qg_ref[...]).astype(BF16)
        ckvn = _rms(proj[:, W1_CKV[0]:W1_CKV[1]], kvg_ref[...]).astype(BF16)
        cos = cos_ref[rows, :]
        sin_lo = sin_lo_ref[rows, :]
        sin_hi = sin_hi_ref[rows, :]
        k_rope = _rope(proj[:, W1_KR[0]:W1_KR[1]], cos, sin_lo, sin_hi)
        qf = jnp.dot(cqn, wq_ref[...], preferred_element_type=F32)
        sin_signed = sin_lo + sin_hi
        kvf = jnp.dot(ckvn, wkv_ref[...], preferred_element_type=F32)
        for h in range(N_HEADS_B):
            lo, hi = h * SLOT, (h + 1) * SLOT
            q_h = qf[:, lo:hi] * cos + qf[:, hs + lo:hs + hi] * sin_signed
            q_ref[rows, lo:hi] = q_h.astype(BF16)
            kv_h = kvf[:, lo:hi]
            k_ref[rows, lo:hi] = jnp.where(low, kv_h, k_rope).astype(BF16)
            v_ref[rows, lo:hi] = jnp.where(low, one_at_zero, kv_h).astype(BF16)


def _proj(x, layer, g, w1, qg, kvg, wq, wkv, rope_tabs, seqs, split):
    t = x.shape[0]
    tm = PROJ_TM
    n0, n1 = seqs[0] // tm, seqs[1] // tm

    def row(cols):
        return pl.BlockSpec((tm, cols), lambda i: (i, 0))

    tab = pl.BlockSpec((tm, SLOT), lambda i: (jnp.where(i < split, i % n0, (i - split) % n1), 0))
    hs = N_HEADS_B * SLOT
    out_cols = (A_Q_COLS, 2 * SLOT, 2 * SLOT, hs, hs, hs)
    return pl.pallas_call(
        _proj_kernel,
        grid=(t // tm,),
        in_specs=[row(D_MODEL), _layer_spec((1, D_MODEL), layer), _const_spec((D_MODEL, W1_COLS)),
                  _layer_spec((1, Q_LORA_RANK), layer), _layer_spec((1, KV_LORA_RANK), layer),
                  _const_spec((Q_LORA_RANK, 2 * hs)), _layer_spec((KV_LORA_RANK, hs), layer),
                  tab, tab, tab],
        out_specs=[row(c) for c in out_cols],
        out_shape=[jax.ShapeDtypeStruct((t, c), BF16) for c in out_cols],
        compiler_params=_params(1),
        name="proj",
    )(x, g, w1, qg, kvg, wq, wkv, *rope_tabs)


def _bias_kernel(rb_ref, bucket_ref, o_ref):
    bucket = bucket_ref[0]
    for h in range(N_HEADS_A):
        acc = jnp.full(bucket.shape, -jnp.inf, F32)
        for b in range(N_BUCKETS):
            acc = jnp.where(bucket == b, rb_ref[b, h], acc)
        o_ref[0, h] = acc


def _bias_table(rel_bias, bucket):
    kw = 3 * BLOCK
    return pl.pallas_call(
        _bias_kernel,
        grid=(3,),
        in_specs=[pl.BlockSpec(memory_space=pltpu.SMEM),
                  pl.BlockSpec((1, BLOCK, kw), lambda e: (e, 0, 0))],
        out_specs=pl.BlockSpec((1, N_HEADS_A, BLOCK, kw), lambda e: (e, 0, 0, 0)),
        out_shape=jax.ShapeDtypeStruct((3, N_HEADS_A, BLOCK, kw), F32),
        compiler_params=_params(1),
        name="t5_bias",
    )(rel_bias, bucket)


def _t5_bucket(rel):
    half = N_BUCKETS // 2
    max_exact = half // 2
    ret = jnp.where(rel > 0, half, 0)
    n = jnp.abs(rel)
    nf = jnp.maximum(n, 1).astype(F32)
    large = max_exact + (jnp.log(nf / max_exact) / math.log(MAX_DISTANCE / max_exact)
                         * (half - max_exact)).astype(jnp.int32)
    large = jnp.minimum(large, half - 1)
    return ret + jnp.where(n < max_exact, n, large)


def _band_buckets():
    qi = jnp.arange(BLOCK, dtype=jnp.int32)[:, None]
    si = jnp.arange(3 * BLOCK, dtype=jnp.int32)[None, :]
    out = []
    for shift in (0, -BLOCK, -2 * BLOCK):
        rel = si + shift - qi
        out.append(jnp.where(jnp.abs(rel) <= WINDOW, _t5_bucket(rel), -1))
    return jnp.stack(out).astype(jnp.int32)


def _win_kernel(q_ref, k_ref, v_ref, bias_ref, sink_ref, o_ref, s0_ref, s1_ref, *, seq):
    step = pl.program_id(1)
    nb = seq // BLOCK
    kw = 3 * BLOCK
    gw = GROUP_A * HEAD_DIM_A
    lane = lax.broadcasted_iota(jnp.int32, (BLOCK, gw), 1)
    keep = [(lane >= r * HEAD_DIM_A) & (lane < (r + 1) * HEAD_DIM_A) for r in range(GROUP_A)]
    s_bufs = (s0_ref, s1_ref)
    row0 = pl.multiple_of(jnp.minimum(step, 0), 8)
    tile_rows = pl.ds(row0, GROUP_A * BLOCK)
    units = [(blk, g) for blk in range(WIN_BLOCKS) for g in range(N_KV_HEADS_A)]
    windows = []
    for blk in range(WIN_BLOCKS):
        n = step * WIN_BLOCKS + blk
        start = pl.multiple_of(jnp.clip((n - 1) * BLOCK, 0, seq - kw), BLOCK)
        windows.append((start, jnp.where(n == 0, 0, jnp.where(n == nb - 1, 2, 1))))

    def scores(u):
        blk, g = units[u]
        start, edge = windows[blk]
        q_grp = q_ref[blk * BLOCK:(blk + 1) * BLOCK, g * gw:(g + 1) * gw]
        q_stack = jnp.concatenate(
            [jnp.where(keep[r], q_grp, jnp.zeros_like(q_grp)) for r in range(GROUP_A)], axis=0)
        k_dup = k_ref[pl.ds(start, kw), g * SLOT:(g + 1) * SLOT]
        k_rep = jnp.concatenate([k_dup, k_dup], axis=1)
        s = lax.dot_general(q_stack, k_rep, (((1,), (1,)), ((), ())),
                            preferred_element_type=F32) + bias_ref[edge, g]
        s_bufs[u % 2][tile_rows, :] = s
        return jnp.max(jnp.concatenate([s, sink_ref[g]], axis=1), axis=-1, keepdims=True)

    def weighted(u, m):
        blk, g = units[u]
        start, _ = windows[blk]
        v_dup = v_ref[pl.ds(start, kw), g * SLOT:(g + 1) * SLOT]
        v_rep = jnp.concatenate([v_dup, v_dup], axis=1)
        p = jnp.exp(s_bufs[u % 2][tile_rows, :] - m)
        denom = (jnp.sum(p, axis=-1, keepdims=True)
                 + jnp.sum(jnp.exp(sink_ref[g] - m), axis=-1, keepdims=True))
        o_all = jnp.dot(p.astype(BF16), v_rep, preferred_element_type=F32) / denom
        out = jnp.zeros((BLOCK, gw), F32)
        for r in range(GROUP_A):
            out = jnp.where(keep[r], o_all[r * BLOCK:(r + 1) * BLOCK], out)
        o_ref[blk * BLOCK:(blk + 1) * BLOCK, g * gw:(g + 1) * gw] = out.astype(BF16)

    m = scores(0)
    for u in range(len(units)):
        m_next = scores(u + 1) if u + 1 < len(units) else None
        weighted(u, m)
        m = m_next


def _stream_block_specs(tq, seq, token0, cols):
    assert token0 % seq == 0 and seq % tq == 0
    nq = seq // tq
    q_spec = pl.BlockSpec((tq, cols), lambda b, i: (token0 // tq + b * nq + i, 0))
    seq_spec = pl.BlockSpec((seq, cols), lambda b, i: (token0 // seq + b, 0))
    return q_spec, seq_spec


def _window_attention(qa, ka, va, bias, sink_rows, batch, seq, token0):
    tq = WIN_BLOCKS * BLOCK
    q_spec, _ = _stream_block_specs(tq, seq, token0, A_Q_COLS)
    _, k_spec = _stream_block_specs(tq, seq, token0, ka.shape[-1])
    _, v_spec = _stream_block_specs(tq, seq, token0, va.shape[-1])
    nq = seq // tq
    return pl.pallas_call(
        functools.partial(_win_kernel, seq=seq),
        grid=(batch, nq),
        in_specs=[q_spec, k_spec, v_spec, _const_spec(bias.shape), _const_spec(sink_rows.shape)],
        out_specs=pl.BlockSpec((tq, A_Q_COLS), lambda b, i: (b * nq + i, 0)),
        out_shape=jax.ShapeDtypeStruct((batch * seq, A_Q_COLS), BF16),
        scratch_shapes=[pltpu.VMEM((GROUP_A * BLOCK, 3 * BLOCK), F32),
                        pltpu.VMEM((GROUP_A * BLOCK, 3 * BLOCK), F32)],
        compiler_params=_params(2),
        name="window_attn",
    )(qa, ka, va, bias, sink_rows)


def _mla_kernel(q_ref, k_ref, v_ref, o_ref, s0_ref, s1_ref, *, seq):
    c2 = (QK_NOPE_DIM + QK_ROPE_DIM) ** -0.5 * math.log2(math.e)
    n_chunks = seq // MLA_KC
    sub = MLA_SUB
    lane = lax.broadcasted_iota(jnp.int32, (sub, SLOT), 1)
    s_bufs = (s0_ref, s1_ref)
    row0 = pl.multiple_of(jnp.minimum(pl.program_id(1), 0), 16)
    tile_rows = pl.ds(row0, sub)
    units = [(t, h) for t in range(q_ref.shape[0] // sub) for h in range(N_HEADS_B)]

    def scores(u):
        t, h = units[u]
        q_h = q_ref[t * sub:(t + 1) * sub, h * SLOT:(h + 1) * SLOT]
        m = jnp.full((sub, 1), -jnp.inf, F32)
        for c in range(n_chunks):
            rows = slice(c * MLA_KC, (c + 1) * MLA_KC)
            s = lax.dot_general(q_h, k_ref[rows, h * SLOT:(h + 1) * SLOT], (((1,), (1,)), ((), ())),
                                preferred_element_type=F32)
            s_bufs[u % 2][tile_rows, rows] = s
            m = jnp.maximum(m, jnp.max(s, axis=-1, keepdims=True))
        return m

    def weighted(u, m):
        _, h = units[u]
        lo = (h // 2) * 2 * SLOT
        acc = jnp.zeros((sub, 2 * SLOT), F32)
        for c in range(n_chunks):
            rows = slice(c * MLA_KC, (c + 1) * MLA_KC)
            p = jnp.exp2((s_bufs[u % 2][tile_rows, rows] - m) * c2)
            acc = acc + jnp.dot(p.astype(BF16), v_ref[rows, lo:lo + 2 * SLOT],
                                preferred_element_type=F32)
        return acc[:, (h % 2) * SLOT:(h % 2 + 1) * SLOT]

    m = scores(0)
    even = None
    for u, (t, h) in enumerate(units):
        m_next = scores(u + 1) if u + 1 < len(units) else None
        acc = weighted(u, m)
        out_h = acc / acc[:, 0:1]
        if h % 2 == 0:
            even = out_h
        else:
            pair = jnp.where(lane < V_DIM_B, pltpu.roll(even, SLOT - V_DIM_B, 1), out_h)
            o_ref[t * sub:(t + 1) * sub, (h // 2) * SLOT:(h // 2 + 1) * SLOT] = pair.astype(BF16)
        m = m_next


def _mla_attention(q, k, v, batch, seq, token0):
    tq = MLA_TQ
    hs = N_HEADS_B * SLOT
    oc = N_HEADS_B * V_DIM_B
    q_spec, seq_spec = _stream_block_specs(tq, seq, token0, hs)
    nq = seq // tq
    return pl.pallas_call(
        functools.partial(_mla_kernel, seq=seq),
        grid=(batch, nq),
        in_specs=[q_spec, seq_spec, seq_spec],
        out_specs=pl.BlockSpec((tq, oc), lambda b, i: (b * nq + i, 0)),
        out_shape=jax.ShapeDtypeStruct((batch * seq, oc), BF16),
        scratch_shapes=[pltpu.VMEM((MLA_SUB, seq), F32), pltpu.VMEM((MLA_SUB, seq), F32)],
        compiler_params=_params(2),
        name="mla_attn",
    )(q, k, v)


def _merge_kernel(x_ref, oa0_ref, oa1_ref, ob0_ref, ob1_ref, pre_ref, post_ref, wga_ref, wgb_ref,
                  wa_ref, wb_ref, wo_ref, o_ref, *, split):
    first = pl.program_id(0) < split
    for sub in range(x_ref.shape[0] // ROW_SUB):
        rows = slice(sub * ROW_SUB, (sub + 1) * ROW_SUB)
        x = x_ref[rows, :]
        hb = _rms(x, pre_ref[...]).astype(BF16)
        oa = _stream_rows((oa0_ref, oa1_ref), rows, first)
        ob = _stream_rows((ob0_ref, ob1_ref), rows, first)
        ga = jax.nn.sigmoid(jnp.dot(hb, wga_ref[...], preferred_element_type=F32))
        merged = ga * jnp.dot(oa, wa_ref[...], preferred_element_type=F32)
        gb = jax.nn.sigmoid(jnp.dot(hb, wgb_ref[...], preferred_element_type=F32))
        merged = merged + gb * jnp.dot(ob, wb_ref[...], preferred_element_type=F32)
        y = jnp.dot(merged.astype(BF16), wo_ref[...], preferred_element_type=F32)
        o_ref[rows, :] = x + _rms(y, post_ref[...])


def _merge(x, oas, obs, layer, pre_g, post_g, wga, wgb, wa, wb, wo, split):
    t = x.shape[0]
    tm = MERGE_TM

    def row(cols):
        return pl.BlockSpec((tm, cols), lambda i: (i, 0))

    bcols = N_HEADS_B * V_DIM_B
    return pl.pallas_call(
        functools.partial(_merge_kernel, split=split),
        grid=(t // tm,),
        in_specs=[row(D_MODEL)] + _stream_specs(tm, A_Q_COLS, split, 2) + _stream_specs(tm, bcols, split, 2)
        + [_layer_spec((1, D_MODEL), layer), _layer_spec((1, D_MODEL), layer),
           _const_spec((D_MODEL, D_MODEL)), _const_spec((D_MODEL, D_MODEL)),
           _layer_spec((A_Q_COLS, D_MODEL), layer), _layer_spec((bcols, D_MODEL), layer),
           _layer_spec((D_MODEL, D_MODEL), layer)],
        out_specs=row(D_MODEL),
        out_shape=jax.ShapeDtypeStruct((t, D_MODEL), F32),
        compiler_params=_params(1),
        name="merge",
    )(x, *oas, *obs, pre_g, post_g, wga, wgb, wa, wb, wo)


def _layer_weights(wi, uq):
    o = 0
    pieces = {}
    for name, sz in (("qa", A_Q_COLS), ("ka", A_KV_COLS), ("va", A_KV_COLS), ("cq", Q_LORA_RANK),
                     ("ckv", KV_LORA_RANK), ("kr", QK_ROPE_DIM), ("ga", D_MODEL), ("gb", D_MODEL)):
        pieces[name] = wi[:, o:o + sz]
        o += sz
    z64 = jnp.zeros((D_MODEL, 64), BF16)
    z32 = jnp.zeros((D_MODEL, 32), BF16)
    w1 = jnp.concatenate([
        pieces["qa"] * (HEAD_DIM_A ** -0.5),
        pieces["ka"], pieces["va"], pieces["cq"], pieces["ckv"],
        z64, pieces["kr"], z32,
    ], axis=1)

    uq = uq.reshape(Q_LORA_RANK, N_HEADS_B, QK_NOPE_DIM + QK_ROPE_DIM)
    q_rope = uq[..., QK_NOPE_DIM:]
    r1, r2 = q_rope[..., :QK_ROPE_DIM // 2], q_rope[..., QK_ROPE_DIM // 2:]
    zq = jnp.zeros((Q_LORA_RANK, N_HEADS_B, SLOT - QK_NOPE_DIM - QK_ROPE_DIM), BF16)
    zn = jnp.zeros((Q_LORA_RANK, N_HEADS_B, QK_NOPE_DIM), BF16)
    q_main = jnp.concatenate([uq, zq], axis=-1).reshape(Q_LORA_RANK, -1)
    q_swap = jnp.concatenate([zn, r2, r1, zq], axis=-1).reshape(Q_LORA_RANK, -1)
    wq = jnp.concatenate([q_main, q_swap], axis=1)
    return w1, wq, pieces["ga"], pieces["gb"]


def _sink_rows(sink_l):
    col = jnp.repeat(sink_l, BLOCK).reshape(N_KV_HEADS_A, GROUP_A * BLOCK, 1)
    lane = jnp.arange(SLOT)[None, None, :]
    return jnp.where(lane == 0, col, -jnp.inf).astype(F32)


def _rope_slot_tables(seq):
    half = QK_ROPE_DIM // 2
    inv_freq = ROPE_THETA ** (-jnp.arange(0, QK_ROPE_DIM, 2, dtype=F32) / QK_ROPE_DIM)
    ang = jnp.arange(seq, dtype=F32)[:, None] * inv_freq[None, :]
    cos, sin = jnp.cos(ang), jnp.sin(ang)
    z_half = jnp.zeros((seq, half), F32)
    z_nope = jnp.zeros((seq, QK_NOPE_DIM), F32)
    pad = jnp.zeros((seq, SLOT - QK_NOPE_DIM - QK_ROPE_DIM), F32)
    cos_tab = jnp.concatenate([jnp.ones((seq, QK_NOPE_DIM), F32), cos, cos, pad], axis=1)
    sin_lo = jnp.concatenate([z_nope, -sin, z_half, pad], axis=1)
    sin_hi = jnp.concatenate([z_nope, z_half, sin, pad], axis=1)
    return cos_tab, sin_lo, sin_hi


def _trunk(x_first, x_second, bias, shared, layers):
    shapes = (x_first.shape, x_second.shape)
    xs = tuple(x.reshape(-1, D_MODEL) for x in (x_first, x_second))
    t0 = xs[0].shape[0]
    assert FFN_TM == PROJ_TM == MERGE_TM and t0 % FFN_TM == 0 and xs[1].shape[0] % FFN_TM == 0
    split = t0 // FFN_TM
    seqs = (shapes[0][1], shapes[1][1])
    streams = ((shapes[0][0], seqs[0], 0), (shapes[1][0], seqs[1], t0))
    rope_tabs = _rope_slot_tables(max(seqs))
    for l, lw in enumerate(layers):
        x = _ffn(xs, l, *shared["ffn1"], split)
        qa, ka, va, q, k, v = _proj(x, l, shared["mix_pre_g"], lw["w1"], shared["q_norm_g"],
                                    shared["kv_norm_g"], lw["wq"], shared["w_ukv"], rope_tabs, seqs, split)
        oas = [_window_attention(qa, ka, va, bias, lw["sink_rows"], *s) for s in streams]
        obs = [_mla_attention(q, k, v, *s) for s in streams]
        x = _merge(x, oas, obs, l, shared["mix_pre_g"], shared["mix_post_g"], lw["wga"], lw["wgb"],
                   shared["w_a_out"], shared["w_b_out"], shared["w_o"], split)
        last = l == len(layers) - 1
        xs = _ffn((x,), l, *shared["ffn2"], split, n_out=2 if last else 1)
        if not last:
            xs = (xs,)
    return tuple(y.reshape(shape) for y, shape in zip(xs, shapes))


def kernel(x_prompt, x_sample, rel_bias, ffn1_pre_g, ffn1_post_g, ffn1_w_gate, ffn1_w_up, ffn1_w_down, mix_pre_g, mix_post_g, w_in, sink, q_norm_g, kv_norm_g, w_uq, w_ukv, w_a_out, w_b_out, w_o, ffn2_pre_g, ffn2_post_g, ffn2_w_gate, ffn2_w_up, ffn2_w_down):
    def gains(g):
        return g.reshape(DEPTH, 1, -1)

    def bf16(w):
        return w.astype(BF16)

    shared = dict(
        ffn1=(gains(ffn1_pre_g), gains(ffn1_post_g), bf16(ffn1_w_gate), bf16(ffn1_w_up), bf16(ffn1_w_down)),
        ffn2=(gains(ffn2_pre_g), gains(ffn2_post_g), bf16(ffn2_w_gate), bf16(ffn2_w_up), bf16(ffn2_w_down)),
        mix_pre_g=gains(mix_pre_g), mix_post_g=gains(mix_post_g),
        q_norm_g=gains(q_norm_g), kv_norm_g=gains(kv_norm_g),
        w_a_out=bf16(w_a_out), w_b_out=bf16(w_b_out), w_o=bf16(w_o), w_ukv=bf16(w_ukv),
    )
    w_in_b, w_uq_b = bf16(w_in), bf16(w_uq)
    layers = []
    for l in range(DEPTH):
        w1, wq, wga, wgb = _layer_weights(w_in_b[l], w_uq_b[l])
        layers.append(dict(w1=w1, wq=wq, wga=wga, wgb=wgb, sink_rows=_sink_rows(sink[l])))
    bias = _bias_table(rel_bias, _band_buckets())
    bias = bias.reshape(3, N_KV_HEADS_A, GROUP_A * BLOCK, 3 * BLOCK)
    return _trunk(x_prompt, x_sample, bias, shared, layers)
```

```python
import functools
import math

import jax
import jax.numpy as jnp
import numpy as np
from jax import lax
from jax.experimental import pallas as pl
from jax.experimental.pallas import tpu as pltpu

F32 = jnp.float32
BF16 = jnp.bfloat16

D_MODEL = 1024
DEPTH = 2
N_HEADS_A = 8
N_KV_HEADS_A = 2
GROUP_A = N_HEADS_A // N_KV_HEADS_A
HEAD_DIM_A = 64
WINDOW = 128
BLOCK = 128
N_HEADS_B = 8
QK_NOPE_DIM = 64
QK_ROPE_DIM = 32
V_DIM_B = 64
Q_LORA_RANK = 384
KV_LORA_RANK = 256
ROPE_THETA = 10000.0
N_BUCKETS = 32
MAX_DISTANCE = 128
D_FF = 2816
FFN_RES_WEIGHT = 0.5
EPS = 1e-6

A_Q_COLS = N_HEADS_A * HEAD_DIM_A
A_KV_COLS = N_KV_HEADS_A * HEAD_DIM_A

LANES = 128
SLOT = LANES
VMEM_LIMIT_BYTES = 56 * 1024 * 1024

W1_QA = (0, 512)
W1_KA = (512, 640)
W1_VA = (640, 768)
W1_CQ = (768, 1152)
W1_CKV = (1152, 1408)
W1_KR = (1408, 1536)
W1_COLS = 1536

FFN_CHUNKS = ((0, 1024), (1024, 1024), (2048, 768))
ROW_SUB = 256
PROJ_SUB = 1024
FFN_TM = 1024
PROJ_TM = 1024
MERGE_TM = 1024
MLA_TQ = 512
MLA_SUB = 256
MLA_KC = 1024
WIN_BLOCKS = 16


def _rms(x, g):
    return x * lax.rsqrt(jnp.mean(x * x, axis=-1, keepdims=True) + EPS) * g


def _const_spec(shape):
    nd = len(shape)
    return pl.BlockSpec(shape, lambda *_: (0,) * nd, pipeline_mode=pl.Buffered(1))


def _layer_spec(shape, layer):
    nd = len(shape)
    return pl.BlockSpec((None,) + tuple(shape), lambda *_: (layer,) + (0,) * nd,
                        pipeline_mode=pl.Buffered(1))


def _params(n_axes, flags=None):
    return pltpu.CompilerParams(
        dimension_semantics=("parallel",) * n_axes,
        vmem_limit_bytes=VMEM_LIMIT_BYTES,
        flags=flags,
    )


def _stream_specs(tm, cols, split, n):
    if n == 1:
        return [pl.BlockSpec((tm, cols), lambda i: (i, 0))]
    return [pl.BlockSpec((tm, cols), lambda i: (jnp.minimum(i, split - 1), 0)),
            pl.BlockSpec((tm, cols), lambda i: (jnp.maximum(i - split, 0), 0))]


def _stream_rows(refs, rows, first):
    if len(refs) == 1:
        return refs[0][rows, :]
    return jnp.where(first, refs[0][rows, :], refs[1][rows, :])


def _ffn_kernel(*refs, n_in, n_out, split):
    x_refs, refs = refs[:n_in], refs[n_in:]
    pre_ref, post_ref, wg_ref, wu_ref, wd_ref = refs[:5]
    o_refs, acc_ref = refs[5:5 + n_out], refs[5 + n_out]
    first = pl.program_id(0) < split
    for sub in range(x_refs[0].shape[0] // ROW_SUB):
        rows = slice(sub * ROW_SUB, (sub + 1) * ROW_SUB)
        x = _stream_rows(x_refs, rows, first)
        hb = _rms(x, pre_ref[...]).astype(BF16)
        for idx, (c0, cw) in enumerate(FFN_CHUNKS):
            g = jnp.dot(hb, wg_ref[:, c0:c0 + cw], preferred_element_type=F32)
            u = jnp.dot(hb, wu_ref[:, c0:c0 + cw], preferred_element_type=F32)
            a = (jax.nn.silu(g) * u).astype(BF16)
            part = jnp.dot(a, wd_ref[c0:c0 + cw, :], preferred_element_type=F32)
            if idx == 0:
                acc_ref[rows, :] = part
            else:
                acc_ref[rows, :] += part
        y = x + FFN_RES_WEIGHT * _rms(acc_ref[rows, :], post_ref[...])
        o_refs[-1][rows, :] = y
    if n_out == 2:
        @pl.when(first)
        def _():
            o_refs[0][...] = o_refs[1][...]


def _ffn(xs, layer, pre_g, post_g, wg, wu, wd, split, n_out=1):
    tm = FFN_TM
    t = sum(x.shape[0] for x in xs)
    n_in = len(xs)
    if n_out == 1:
        out_shape = [jax.ShapeDtypeStruct((t, D_MODEL), F32)]
    else:
        out_shape = [jax.ShapeDtypeStruct((split * tm, D_MODEL), F32),
                     jax.ShapeDtypeStruct((t - split * tm, D_MODEL), F32)]
    outs = pl.pallas_call(
        functools.partial(_ffn_kernel, n_in=n_in, n_out=n_out, split=split),
        grid=(t // tm,),
        in_specs=_stream_specs(tm, D_MODEL, split, n_in)
        + [_layer_spec((1, D_MODEL), layer), _layer_spec((1, D_MODEL), layer),
           _layer_spec((D_MODEL, D_FF), layer), _layer_spec((D_MODEL, D_FF), layer),
           _layer_spec((D_FF, D_MODEL), layer)],
        out_specs=_stream_specs(tm, D_MODEL, split, n_out),
        out_shape=out_shape,
        scratch_shapes=[pltpu.VMEM((tm, D_MODEL), F32)],
        compiler_params=pltpu.CompilerParams(dimension_semantics=("arbitrary",),
                                             vmem_limit_bytes=VMEM_LIMIT_BYTES),
        name="ffn",
    )(*xs, pre_g, post_g, wg, wu, wd)
    return outs[0] if n_out == 1 else outs


def _rope(x, cos, sin_lo, sin_hi):
    half = QK_ROPE_DIM // 2
    return x * cos + pltpu.roll(x, SLOT - half, 1) * sin_lo + pltpu.roll(x, half, 1) * sin_hi


def _proj_kernel(x_ref, g_ref, w1_ref, qg_ref, kvg_ref, wq_ref, wkv_ref, cos_ref, sin_lo_ref, sin_hi_ref,
                 qa_ref, ka_ref, va_ref, q_ref, k_ref, v_ref):
    hs = N_HEADS_B * SLOT
    half = SLOT // 2
    lane = lax.broadcasted_iota(jnp.int32, (PROJ_SUB, SLOT), 1)
    low = lane < half
    one_at_zero = jnp.where(lane == 0, 1.0, 0.0)

    def dup_halves(x):
        swapped = pltpu.roll(x, half, 1)
        return jnp.where(low, x, swapped), jnp.where(low, swapped, x)

    for sub in range(x_ref.shape[0] // PROJ_SUB):
        rows = slice(sub * PROJ_SUB, (sub + 1) * PROJ_SUB)
        hb = _rms(x_ref[rows, :], g_ref[...]).astype(BF16)
        proj = jnp.dot(hb, w1_ref[...], preferred_element_type=F32)
        qa_ref[rows, :] = proj[:, W1_QA[0]:W1_QA[1]].astype(BF16)
        for src, dst in ((W1_KA, ka_ref), (W1_VA, va_ref)):
            first, second = dup_halves(proj[:, src[0]:src[1]])
            dst[rows, :SLOT] = first.astype(BF16)
            dst[rows, SLOT:] = second.astype(BF16)
        cqn = _rms(proj[:, W1_CQ[0]:W1_CQ[1]], qg_ref[...]).astype(BF16)
        ckvn = _rms(proj[:, W1_CKV[0]:W1_CKV[1]], kvg_ref[...]).astype(BF16)
        cos = cos_ref[rows, :]
        sin_lo = sin_lo_ref[rows, :]
        sin_hi = sin_hi_ref[rows, :]
        k_rope = _rope(proj[:, W1_KR[0]:W1_KR[1]], cos, sin_lo, sin_hi)
        qf = jnp.dot(cqn, wq_ref[...], preferred_element_type=F32)
        sin_signed = sin_lo + sin_hi
        kvf = jnp.dot(ckvn, wkv_ref[...], preferred_element_type=F32)
        for h in range(N_HEADS_B):
            lo, hi = h * SLOT, (h + 1) * SLOT
            q_h = qf[:, lo:hi] * cos + qf[:, hs + lo:hs + hi] * sin_signed
            q_ref[rows, lo:hi] = q_h.astype(BF16)
            kv_h = kvf[:, lo:hi]
            k_ref[rows, lo:hi] = jnp.where(low, kv_h, k_rope).astype(BF16)
            v_ref[rows, lo:hi] = jnp.where(low, one_at_zero, kv_h).astype(BF16)


def _proj(x, layer, g, w1, qg, kvg, wq, wkv, rope_tabs, seqs, split):
    t = x.shape[0]
    tm = PROJ_TM
    n0, n1 = seqs[0] // tm, seqs[1] // tm

    def row(cols):
        return pl.BlockSpec((tm, cols), lambda i: (i, 0))

    tab = pl.BlockSpec((tm, SLOT), lambda i: (jnp.where(i < split, i % n0, (i - split) % n1), 0))
    hs = N_HEADS_B * SLOT
    out_cols = (A_Q_COLS, 2 * SLOT, 2 * SLOT, hs, hs, hs)
    return pl.pallas_call(
        _proj_kernel,
        grid=(t // tm,),
        in_specs=[row(D_MODEL), _layer_spec((1, D_MODEL), layer), _const_spec((D_MODEL, W1_COLS)),
                  _layer_spec((1, Q_LORA_RANK), layer), _layer_spec((1, KV_LORA_RANK), layer),
                  _const_spec((Q_LORA_RANK, 2 * hs)), _layer_spec((KV_LORA_RANK, hs), layer),
                  tab, tab, tab],
        out_specs=[row(c) for c in out_cols],
        out_shape=[jax.ShapeDtypeStruct((t, c), BF16) for c in out_cols],
        compiler_params=_params(1),
        name="proj",
    )(x, g, w1, qg, kvg, wq, wkv, *rope_tabs)


def _bias_kernel(rb_ref, bucket_ref, o_ref):
    bucket = bucket_ref[0]
    for h in range(N_HEADS_A):
        acc = jnp.full(bucket.shape, -jnp.inf, F32)
        for b in range(N_BUCKETS):
            acc = jnp.where(bucket == b, rb_ref[b, h], acc)
        o_ref[0, h] = acc


def _bias_table(rel_bias, bucket):
    kw = 3 * BLOCK
    return pl.pallas_call(
        _bias_kernel,
        grid=(3,),
        in_specs=[pl.BlockSpec(memory_space=pltpu.SMEM),
                  pl.BlockSpec((1, BLOCK, kw), lambda e: (e, 0, 0))],
        out_specs=pl.BlockSpec((1, N_HEADS_A, BLOCK, kw), lambda e: (e, 0, 0, 0)),
        out_shape=jax.ShapeDtypeStruct((3, N_HEADS_A, BLOCK, kw), F32),
        compiler_params=_params(1),
        name="t5_bias",
    )(rel_bias, bucket)


def _t5_bucket(rel):
    half = N_BUCKETS // 2
    max_exact = half // 2
    ret = jnp.where(rel > 0, half, 0)
    n = jnp.abs(rel)
    nf = jnp.maximum(n, 1).astype(F32)
    large = max_exact + (jnp.log(nf / max_exact) / math.log(MAX_DISTANCE / max_exact)
                         * (half - max_exact)).astype(jnp.int32)
    large = jnp.minimum(large, half - 1)
    return ret + jnp.where(n < max_exact, n, large)


def _band_buckets():
    qi = jnp.arange(BLOCK, dtype=jnp.int32)[:, None]
    si = jnp.arange(3 * BLOCK, dtype=jnp.int32)[None, :]
    out = []
    for shift in (0, -BLOCK, -2 * BLOCK):
        rel = si + shift - qi
        out.append(jnp.where(jnp.abs(rel) <= WINDOW, _t5_bucket(rel), -1))
    return jnp.stack(out).astype(jnp.int32)


def _win_kernel(q_ref, k_ref, v_ref, bias_ref, sink_ref, o_ref, s0_ref, s1_ref, *, seq):
    step = pl.program_id(1)
    nb = seq // BLOCK
    kw = 3 * BLOCK
    gw = GROUP_A * HEAD_DIM_A
    lane = lax.broadcasted_iota(jnp.int32, (BLOCK, gw), 1)
    keep = [(lane >= r * HEAD_DIM_A) & (lane < (r + 1) * HEAD_DIM_A) for r in range(GROUP_A)]
    s_bufs = (s0_ref, s1_ref)
    row0 = pl.multiple_of(jnp.minimum(step, 0), 8)
    tile_rows = pl.ds(row0, GROUP_A * BLOCK)
    units = [(blk, g) for blk in range(WIN_BLOCKS) for g in range(N_KV_HEADS_A)]
    windows = []
    for blk in range(WIN_BLOCKS):
        n = step * WIN_BLOCKS + blk
        start = pl.multiple_of(jnp.clip((n - 1) * BLOCK, 0, seq - kw), BLOCK)
        windows.append((start, jnp.where(n == 0, 0, jnp.where(n == nb - 1, 2, 1))))

    def scores(u):
        blk, g = units[u]
        start, edge = windows[blk]
        q_grp = q_ref[blk * BLOCK:(blk + 1) * BLOCK, g * gw:(g + 1) * gw]
        q_stack = jnp.concatenate(
            [jnp.where(keep[r], q_grp, jnp.zeros_like(q_grp)) for r in range(GROUP_A)], axis=0)
        k_dup = k_ref[pl.ds(start, kw), g * SLOT:(g + 1) * SLOT]
        k_rep = jnp.concatenate([k_dup, k_dup], axis=1)
        s = lax.dot_general(q_stack, k_rep, (((1,), (1,)), ((), ())),
                            preferred_element_type=F32) + bias_ref[edge, g]
        s_bufs[u % 2][tile_rows, :] = s
        return jnp.max(jnp.concatenate([s, sink_ref[g]], axis=1), axis=-1, keepdims=True)

    def weighted(u, m):
        blk, g = units[u]
        start, _ = windows[blk]
        v_dup = v_ref[pl.ds(start, kw), g * SLOT:(g + 1) * SLOT]
        v_rep = jnp.concatenate([v_dup, v_dup], axis=1)
        p = jnp.exp(s_bufs[u % 2][tile_rows, :] - m)
        denom = (jnp.sum(p, axis=-1, keepdims=True)
                 + jnp.sum(jnp.exp(sink_ref[g] - m), axis=-1, keepdims=True))
        o_all = jnp.dot(p.astype(BF16), v_rep, preferred_element_type=F32) / denom
        out = jnp.zeros((BLOCK, gw), F32)
        for r in range(GROUP_A):
            out = jnp.where(keep[r], o_all[r * BLOCK:(r + 1) * BLOCK], out)
        o_ref[blk * BLOCK:(blk + 1) * BLOCK, g * gw:(g + 1) * gw] = out.astype(BF16)

    m = scores(0)
    for u in range(len(units)):
        m_next = scores(u + 1) if u + 1 < len(units) else None
        weighted(u, m)
        m = m_next


def _stream_block_specs(tq, seq, token0, cols):
    assert token0 % seq == 0 and seq % tq == 0
    nq = seq // tq
    q_spec = pl.BlockSpec((tq, cols), lambda b, i: (token0 // tq + b * nq + i, 0))
    seq_spec = pl.BlockSpec((seq, cols), lambda b, i: (token0 // seq + b, 0))
    return q_spec, seq_spec


def _window_attention(qa, ka, va, bias, sink_rows, batch, seq, token0):
    tq = WIN_BLOCKS * BLOCK
    q_spec, _ = _stream_block_specs(tq, seq, token0, A_Q_COLS)
    _, k_spec = _stream_block_specs(tq, seq, token0, ka.shape[-1])
    _, v_spec = _stream_block_specs(tq, seq, token0, va.shape[-1])
    nq = seq // tq
    return pl.pallas_call(
        functools.partial(_win_kernel, seq=seq),
        grid=(batch, nq),
        in_specs=[q_spec, k_spec, v_spec, _const_spec(bias.shape), _const_spec(sink_rows.shape)],
        out_specs=pl.BlockSpec((tq, A_Q_COLS), lambda b, i: (b * nq + i, 0)),
        out_shape=jax.ShapeDtypeStruct((batch * seq, A_Q_COLS), BF16),
        scratch_shapes=[pltpu.VMEM((GROUP_A * BLOCK, 3 * BLOCK), F32),
                        pltpu.VMEM((GROUP_A * BLOCK, 3 * BLOCK), F32)],
        compiler_params=_params(2),
        name="window_attn",
    )(qa, ka, va, bias, sink_rows)


def _mla_kernel(q_ref, k_ref, v_ref, o_ref, s0_ref, s1_ref, *, seq):
    c2 = (QK_NOPE_DIM + QK_ROPE_DIM) ** -0.5 * math.log2(math.e)
    n_chunks = seq // MLA_KC
    sub = MLA_SUB
    lane = lax.broadcasted_iota(jnp.int32, (sub, SLOT), 1)
    s_bufs = (s0_ref, s1_ref)
    row0 = pl.multiple_of(jnp.minimum(pl.program_id(1), 0), 16)
    tile_rows = pl.ds(row0, sub)
    units = [(t, h) for t in range(q_ref.shape[0] // sub) for h in range(N_HEADS_B)]

    def scores(u):
        t, h = units[u]
        q_h = q_ref[t * sub:(t + 1) * sub, h * SLOT:(h + 1) * SLOT]
        m = jnp.full((sub, 1), -jnp.inf, F32)
        for c in range(n_chunks):
            rows = slice(c * MLA_KC, (c + 1) * MLA_KC)
            s = lax.dot_general(q_h, k_ref[rows, h * SLOT:(h + 1) * SLOT], (((1,), (1,)), ((), ())),
                                preferred_element_type=F32)
            s_bufs[u % 2][tile_rows, rows] = s
            m = jnp.maximum(m, jnp.max(s, axis=-1, keepdims=True))
        return m

    def weighted(u, m):
        _, h = units[u]
        lo = (h // 2) * 2 * SLOT
        acc = jnp.zeros((sub, 2 * SLOT), F32)
        for c in range(n_chunks):
            rows = slice(c * MLA_KC, (c + 1) * MLA_KC)
            p = jnp.exp2((s_bufs[u % 2][tile_rows, rows] - m) * c2)
            acc = acc + jnp.dot(p.astype(BF16), v_ref[rows, lo:lo + 2 * SLOT],
                                preferred_element_type=F32)
        return acc[:, (h % 2) * SLOT:(h % 2 + 1) * SLOT]

    m = scores(0)
    even = None
    for u, (t, h) in enumerate(units):
        m_next = scores(u + 1) if u + 1 < len(units) else None
        acc = weighted(u, m)
        out_h = acc / acc[:, 0:1]
        if h % 2 == 0:
            even = out_h
        else:
            pair = jnp.where(lane < V_DIM_B, pltpu.roll(even, SLOT - V_DIM_B, 1), out_h)
            o_ref[t * sub:(t + 1) * sub, (h // 2) * SLOT:(h // 2 + 1) * SLOT] = pair.astype(BF16)
        m = m_next


def _mla_attention(q, k, v, batch, seq, token0):
    tq = MLA_TQ
    hs = N_HEADS_B * SLOT
    oc = N_HEADS_B * V_DIM_B
    q_spec, seq_spec = _stream_block_specs(tq, seq, token0, hs)
    nq = seq // tq
    return pl.pallas_call(
        functools.partial(_mla_kernel, seq=seq),
        grid=(batch, nq),
        in_specs=[q_spec, seq_spec, seq_spec],
        out_specs=pl.BlockSpec((tq, oc), lambda b, i: (b * nq + i, 0)),
        out_shape=jax.ShapeDtypeStruct((batch * seq, oc), BF16),
        scratch_shapes=[pltpu.VMEM((MLA_SUB, seq), F32), pltpu.VMEM((MLA_SUB, seq), F32)],
        compiler_params=_params(2),
        name="mla_attn",
    )(q, k, v)


def _merge_kernel(x_ref, oa0_ref, oa1_ref, ob0_ref, ob1_ref, pre_ref, post_ref, wga_ref, wgb_ref,
                  wa_ref, wb_ref, wo_ref, o_ref, *, split):
    first = pl.program_id(0) < split
    for sub in range(x_ref.shape[0] // ROW_SUB):
        rows = slice(sub * ROW_SUB, (sub + 1) * ROW_SUB)
        x = x_ref[rows, :]
        hb = _rms(x, pre_ref[...]).astype(BF16)
        oa = _stream_rows((oa0_ref, oa1_ref), rows, first)
        ob = _stream_rows((ob0_ref, ob1_ref), rows, first)
        ga = jax.nn.sigmoid(jnp.dot(hb, wga_ref[...], preferred_element_type=F32))
        merged = ga * jnp.dot(oa, wa_ref[...], preferred_element_type=F32)
        gb = jax.nn.sigmoid(jnp.dot(hb, wgb_ref[...], preferred_element_type=F32))
        merged = merged + gb * jnp.dot(ob, wb_ref[...], preferred_element_type=F32)
        y = jnp.dot(merged.astype(BF16), wo_ref[...], preferred_element_type=F32)
        o_ref[rows, :] = x + _rms(y, post_ref[...])


def _merge(x, oas, obs, layer, pre_g, post_g, wga, wgb, wa, wb, wo, split):
    t = x.shape[0]
    tm = MERGE_TM

    def row(cols):
        return pl.BlockSpec((tm, cols), lambda i: (i, 0))

    bcols = N_HEADS_B * V_DIM_B
    return pl.pallas_call(
        functools.partial(_merge_kernel, split=split),
        grid=(t // tm,),
        in_specs=[row(D_MODEL)] + _stream_specs(tm, A_Q_COLS, split, 2) + _stream_specs(tm, bcols, split, 2)
        + [_layer_spec((1, D_MODEL), layer), _layer_spec((1, D_MODEL), layer),
           _const_spec((D_MODEL, D_MODEL)), _const_spec((D_MODEL, D_MODEL)),
           _layer_spec((A_Q_COLS, D_MODEL), layer), _layer_spec((bcols, D_MODEL), layer),
           _layer_spec((D_MODEL, D_MODEL), layer)],
        out_specs=row(D_MODEL),
        out_shape=jax.ShapeDtypeStruct((t, D_MODEL), F32),
        compiler_params=_params(1),
        name="merge",
    )(x, *oas, *obs, pre_g, post_g, wga, wgb, wa, wb, wo)


def _layer_weights(wi, uq):
    o = 0
    pieces = {}
    for name, sz in (("qa", A_Q_COLS), ("ka", A_KV_COLS), ("va", A_KV_COLS), ("cq", Q_LORA_RANK),
                     ("ckv", KV_LORA_RANK), ("kr", QK_ROPE_DIM), ("ga", D_MODEL), ("gb", D_MODEL)):
        pieces[name] = wi[:, o:o + sz]
        o += sz
    z64 = jnp.zeros((D_MODEL, 64), BF16)
    z32 = jnp.zeros((D_MODEL, 32), BF16)
    w1 = jnp.concatenate([
        pieces["qa"] * (HEAD_DIM_A ** -0.5),
        pieces["ka"], pieces["va"], pieces["cq"], pieces["ckv"],
        z64, pieces["kr"], z32,
    ], axis=1)

    uq = uq.reshape(Q_LORA_RANK, N_HEADS_B, QK_NOPE_DIM + QK_ROPE_DIM)
    q_rope = uq[..., QK_NOPE_DIM:]
    r1, r2 = q_rope[..., :QK_ROPE_DIM // 2], q_rope[..., QK_ROPE_DIM // 2:]
    zq = jnp.zeros((Q_LORA_RANK, N_HEADS_B, SLOT - QK_NOPE_DIM - QK_ROPE_DIM), BF16)
    zn = jnp.zeros((Q_LORA_RANK, N_HEADS_B, QK_NOPE_DIM), BF16)
    q_main = jnp.concatenate([uq, zq], axis=-1).reshape(Q_LORA_RANK, -1)
    q_swap = jnp.concatenate([zn, r2, r1, zq], axis=-1).reshape(Q_LORA_RANK, -1)
    wq = jnp.concatenate([q_main, q_swap], axis=1)
    return w1, wq, pieces["ga"], pieces["gb"]


def _sink_rows(sink_l):
    col = jnp.repeat(sink_l, BLOCK).reshape(N_KV_HEADS_A, GROUP_A * BLOCK, 1)
    lane = jnp.arange(SLOT)[None, None, :]
    return jnp.where(lane == 0, col, -jnp.inf).astype(F32)


def _rope_slot_tables(seq):
    half = QK_ROPE_DIM // 2
    inv_freq = ROPE_THETA ** (-np.arange(0, QK_ROPE_DIM, 2, dtype=np.float64) / QK_ROPE_DIM)
    ang = np.arange(seq, dtype=np.float64)[:, None] * inv_freq[None, :]
    cos, sin = np.cos(ang), np.sin(ang)
    z_half = np.zeros((seq, half))
    z_nope = np.zeros((seq, QK_NOPE_DIM))
    pad = np.zeros((seq, SLOT - QK_NOPE_DIM - QK_ROPE_DIM))
    cos_tab = np.concatenate([np.ones((seq, QK_NOPE_DIM)), cos, cos, pad], axis=1)
    sin_lo = np.concatenate([z_nope, -sin, z_half, pad], axis=1)
    sin_hi = np.concatenate([z_nope, z_half, sin, pad], axis=1)
    return tuple(jnp.asarray(t, dtype=F32) for t in (cos_tab, sin_lo, sin_hi))


def _trunk(x_first, x_second, bias, shared, layers):
    shapes = (x_first.shape, x_second.shape)
    xs = tuple(x.reshape(-1, D_MODEL) for x in (x_first, x_second))
    t0 = xs[0].shape[0]
    assert FFN_TM == PROJ_TM == MERGE_TM and t0 % FFN_TM == 0 and xs[1].shape[0] % FFN_TM == 0
    split = t0 // FFN_TM
    seqs = (shapes[0][1], shapes[1][1])
    streams = ((shapes[0][0], seqs[0], 0), (shapes[1][0], seqs[1], t0))
    rope_tabs = _rope_slot_tables(max(seqs))
    for l, lw in enumerate(layers):
        x = _ffn(xs, l, *shared["ffn1"], split)
        qa, ka, va, q, k, v = _proj(x, l, shared["mix_pre_g"], lw["w1"], shared["q_norm_g"],
                                    shared["kv_norm_g"], lw["wq"], shared["w_ukv"], rope_tabs, seqs, split)
        oas = [_window_attention(qa, ka, va, bias, lw["sink_rows"], *s) for s in streams]
        obs = [_mla_attention(q, k, v, *s) for s in streams]
        x = _merge(x, oas, obs, l, shared["mix_pre_g"], shared["mix_post_g"], lw["wga"], lw["wgb"],
                   shared["w_a_out"], shared["w_b_out"], shared["w_o"], split)
        last = l == len(layers) - 1
        xs = _ffn((x,), l, *shared["ffn2"], split, n_out=2 if last else 1)
        if not last:
            xs = (xs,)
    return tuple(y.reshape(shape) for y, shape in zip(xs, shapes))


def kernel(x_prompt, x_sample, rel_bias, ffn1_pre_g, ffn1_post_g, ffn1_w_gate, ffn1_w_up, ffn1_w_down, mix_pre_g, mix_post_g, w_in, sink, q_norm_g, kv_norm_g, w_uq, w_ukv, w_a_out, w_b_out, w_o, ffn2_pre_g, ffn2_post_g, ffn2_w_gate, ffn2_w_up, ffn2_w_down):
    def gains(g):
        return g.reshape(DEPTH, 1, -1)

    def bf16(w):
        return w.astype(BF16)

    shared = dict(
        ffn1=(gains(ffn1_pre_g), gains(ffn1_post_g), bf16(ffn1_w_gate), bf16(ffn1_w_up), bf16(ffn1_w_down)),
        ffn2=(gains(ffn2_pre_g), gains(ffn2_post_g), bf16(ffn2_w_gate), bf16(ffn2_w_up), bf16(ffn2_w_down)),
        mix_pre_g=gains(mix_pre_g), mix_post_g=gains(mix_post_g),
        q_norm_g=gains(q_norm_g), kv_norm_g=gains(kv_norm_g),
        w_a_out=bf16(w_a_out), w_b_out=bf16(w_b_out), w_o=bf16(w_o), w_ukv=bf16(w_ukv),
    )
    w_in_b, w_uq_b = bf16(w_in), bf16(w_uq)
    layers = []
    for l in range(DEPTH):
        w1, wq, wga, wgb = _layer_weights(w_in_b[l], w_uq_b[l])
        layers.append(dict(w1=w1, wq=wq, wga=wga, wgb=wgb, sink_rows=_sink_rows(sink[l])))
    bias = _bias_table(rel_bias, _band_buckets())
    bias = bias.reshape(3, N_KV_HEADS_A, GROUP_A * BLOCK, 3 * BLOCK)
    return _trunk(x_prompt, x_sample, bias, shared, layers)
```

```python
import functools
import math

import jax
import jax.numpy as jnp
import numpy as np
from jax import lax
from jax.experimental import pallas as pl
from jax.experimental.pallas import tpu as pltpu

F32 = jnp.float32
BF16 = jnp.bfloat16

D_MODEL = 1024
DEPTH = 2
N_HEADS_A = 8
N_KV_HEADS_A = 2
GROUP_A = N_HEADS_A // N_KV_HEADS_A
HEAD_DIM_A = 64
WINDOW = 128
BLOCK = 128
N_HEADS_B = 8
QK_NOPE_DIM = 64
QK_ROPE_DIM = 32
V_DIM_B = 64
Q_LORA_RANK = 384
KV_LORA_RANK = 256
ROPE_THETA = 10000.0
N_BUCKETS = 32
MAX_DISTANCE = 128
D_FF = 2816
FFN_RES_WEIGHT = 0.5
EPS = 1e-6

A_Q_COLS = N_HEADS_A * HEAD_DIM_A
A_KV_COLS = N_KV_HEADS_A * HEAD_DIM_A

LANES = 128
SLOT = LANES
VMEM_LIMIT_BYTES = 56 * 1024 * 1024

W1_QA = (0, 512)
W1_KA = (512, 640)
W1_VA = (640, 768)
W1_CQ = (768, 1152)
W1_CKV = (1152, 1408)
W1_KR = (1408, 1536)
W1_COLS = 1536

FFN_CHUNKS = ((0, 1024), (1024, 1024), (2048, 768))
FFN_WEIGHT_PIECES = 16
ROW_SUB = 256
MERGE_SUB = 256
PROJ_SUB = 1024
FFN_TM = 1024
PROJ_TM = 1024
MERGE_TM = 1024
MLA_TQ = 512
MLA_SUB = 256
MLA_KC = 1024
WIN_BLOCKS = 16


def _rms(x, g):
    return x * lax.rsqrt(jnp.mean(x * x, axis=-1, keepdims=True) + EPS) * g


def _const_spec(shape):
    nd = len(shape)
    return pl.BlockSpec(shape, lambda *_: (0,) * nd, pipeline_mode=pl.Buffered(1))


def _layer_spec(shape, layer):
    nd = len(shape)
    return pl.BlockSpec((None,) + tuple(shape), lambda *_: (layer,) + (0,) * nd,
                        pipeline_mode=pl.Buffered(1))


def _params(n_axes, flags=None):
    return pltpu.CompilerParams(
        dimension_semantics=("parallel",) * n_axes,
        vmem_limit_bytes=VMEM_LIMIT_BYTES,
        flags=flags,
    )


def _stream_specs(tm, cols, split, n):
    if n == 1:
        return [pl.BlockSpec((tm, cols), lambda i: (i, 0))]
    return [pl.BlockSpec((tm, cols), lambda i: (jnp.minimum(i, split - 1), 0)),
            pl.BlockSpec((tm, cols), lambda i: (jnp.maximum(i - split, 0), 0))]


def _stream_rows(refs, rows, first):
    if len(refs) == 1:
        return refs[0][rows, :]
    return jnp.where(first, refs[0][rows, :], refs[1][rows, :])


def _load_weight_bf16(w_hbm, w_ref, stage_ref, sem_ref):
    rows = stage_ref.shape[1]
    n_pieces = w_ref.shape[0] // rows

    def piece_copy(k):
        slot = k % 2
        return pltpu.make_async_copy(w_hbm.at[pl.ds(k * rows, rows), :], stage_ref.at[slot],
                                     sem_ref.at[slot])

    piece_copy(0).start()
    for k in range(n_pieces):
        if k + 1 < n_pieces:
            piece_copy(k + 1).start()
        piece_copy(k).wait()
        w_ref[k * rows:(k + 1) * rows, :] = stage_ref[k % 2].astype(BF16)


def _ffn_kernel(*refs, n_in, n_out, split, layer):
    x_refs, refs = refs[:n_in], refs[n_in:]
    pre_ref, post_ref, wg_hbm, wu_hbm, wd_hbm = refs[:5]
    o_refs, refs = refs[5:5 + n_out], refs[5 + n_out:]
    acc_ref, wg_ref, wu_ref, wd_ref, stage_in_ref, stage_out_ref, sem_ref = refs
    first = pl.program_id(0) < split

    @pl.when(pl.program_id(0) == 0)
    def _():
        _load_weight_bf16(wg_hbm.at[layer], wg_ref, stage_in_ref, sem_ref)
        _load_weight_bf16(wu_hbm.at[layer], wu_ref, stage_in_ref, sem_ref)
        _load_weight_bf16(wd_hbm.at[layer], wd_ref, stage_out_ref, sem_ref)

    for sub in range(x_refs[0].shape[0] // ROW_SUB):
        rows = slice(sub * ROW_SUB, (sub + 1) * ROW_SUB)
        x = _stream_rows(x_refs, rows, first)
        hb = _rms(x, pre_ref[...]).astype(BF16)
        for idx, (c0, cw) in enumerate(FFN_CHUNKS):
            g = jnp.dot(hb, wg_ref[:, c0:c0 + cw], preferred_element_type=F32)
            u = jnp.dot(hb, wu_ref[:, c0:c0 + cw], preferred_element_type=F32)
            a = (jax.nn.silu(g) * u).astype(BF16)
            part = jnp.dot(a, wd_ref[c0:c0 + cw, :], preferred_element_type=F32)
            if idx == 0:
                acc_ref[rows, :] = part
            else:
                acc_ref[rows, :] += part
        y = x + FFN_RES_WEIGHT * _rms(acc_ref[rows, :], post_ref[...])
        o_refs[-1][rows, :] = y
    if n_out == 2:
        @pl.when(first)
        def _():
            o_refs[0][...] = o_refs[1][...]


def _ffn(xs, layer, pre_g, post_g, wg, wu, wd, split, n_out=1):
    tm = FFN_TM
    t = sum(x.shape[0] for x in xs)
    n_in = len(xs)
    hbm = pl.BlockSpec(memory_space=pl.ANY)
    if n_out == 1:
        out_shape = [jax.ShapeDtypeStruct((t, D_MODEL), F32)]
    else:
        out_shape = [jax.ShapeDtypeStruct((split * tm, D_MODEL), F32),
                     jax.ShapeDtypeStruct((t - split * tm, D_MODEL), F32)]
    outs = pl.pallas_call(
        functools.partial(_ffn_kernel, n_in=n_in, n_out=n_out, split=split, layer=layer),
        grid=(t // tm,),
        in_specs=_stream_specs(tm, D_MODEL, split, n_in)
        + [_layer_spec((1, D_MODEL), layer), _layer_spec((1, D_MODEL), layer), hbm, hbm, hbm],
        out_specs=_stream_specs(tm, D_MODEL, split, n_out),
        out_shape=out_shape,
        scratch_shapes=[pltpu.VMEM((tm, D_MODEL), F32),
                        pltpu.VMEM((D_MODEL, D_FF), BF16), pltpu.VMEM((D_MODEL, D_FF), BF16),
                        pltpu.VMEM((D_FF, D_MODEL), BF16),
                        pltpu.VMEM((2, D_MODEL // FFN_WEIGHT_PIECES, D_FF), F32),
                        pltpu.VMEM((2, D_FF // FFN_WEIGHT_PIECES, D_MODEL), F32),
                        pltpu.SemaphoreType.DMA((2,))],
        compiler_params=pltpu.CompilerParams(dimension_semantics=("arbitrary",),
                                             vmem_limit_bytes=VMEM_LIMIT_BYTES),
        name="ffn",
    )(*xs, pre_g, post_g, wg, wu, wd)
    return outs[0] if n_out == 1 else outs


def _rope(x, cos, sin_lo, sin_hi):
    half = QK_ROPE_DIM // 2
    return x * cos + pltpu.roll(x, SLOT - half, 1) * sin_lo + pltpu.roll(x, half, 1) * sin_hi


def _proj_kernel(x_ref, g_ref, w1_ref, qg_ref, kvg_ref, wq_ref, wkv_ref, cos_ref, sin_lo_ref, sin_hi_ref,
                 qa_ref, ka_ref, va_ref, q_ref, k_ref, v_ref):
    hs = N_HEADS_B * SLOT
    half = SLOT // 2
    lane = lax.broadcasted_iota(jnp.int32, (PROJ_SUB, SLOT), 1)
    low = lane < half
    one_at_zero = jnp.where(lane == 0, 1.0, 0.0)

    def dup_halves(x):
        swapped = pltpu.roll(x, half, 1)
        return jnp.where(low, x, swapped), jnp.where(low, swapped, x)

    for sub in range(x_ref.shape[0] // PROJ_SUB):
        rows = slice(sub * PROJ_SUB, (sub + 1) * PROJ_SUB)
        hb = _rms(x_ref[rows, :], g_ref[...]).astype(BF16)
        proj = jnp.dot(hb, w1_ref[...], preferred_element_type=F32)
        qa_ref[rows, :] = proj[:, W1_QA[0]:W1_QA[1]].astype(BF16)
        for src, dst in ((W1_KA, ka_ref), (W1_VA, va_ref)):
            first, second = dup_halves(proj[:, src[0]:src[1]])
            dst[rows, :SLOT] = first.astype(BF16)
            dst[rows, SLOT:] = second.astype(BF16)
        cqn = _rms(proj[:, W1_CQ[0]:W1_CQ[1]], qg_ref[...]).astype(BF16)
        ckvn = _rms(proj[:, W1_CKV[0]:W1_CKV[1]], kvg_ref[...]).astype(BF16)
        cos = cos_ref[rows, :]
        sin_lo = sin_lo_ref[rows, :]
        sin_hi = sin_hi_ref[rows, :]
        k_rope = _rope(proj[:, W1_KR[0]:W1_KR[1]], cos, sin_lo, sin_hi)
        qf = jnp.dot(cqn, wq_ref[...], preferred_element_type=F32)
        sin_signed = sin_lo + sin_hi
        kvf = jnp.dot(ckvn, wkv_ref[...], preferred_element_type=F32)
        for h in range(N_HEADS_B):
            lo, hi = h * SLOT, (h + 1) * SLOT
            q_h = qf[:, lo:hi] * cos + qf[:, hs + lo:hs + hi] * sin_signed
            q_ref[rows, lo:hi] = q_h.astype(BF16)
            kv_h = kvf[:, lo:hi]
            k_ref[rows, lo:hi] = jnp.where(low, kv_h, k_rope).astype(BF16)
            v_ref[rows, lo:hi] = jnp.where(low, one_at_zero, kv_h).astype(BF16)


def _proj(x, layer, g, w1, qg, kvg, wq, wkv, rope_tabs, seqs, split):
    t = x.shape[0]
    tm = PROJ_TM
    n0, n1 = seqs[0] // tm, seqs[1] // tm

    def row(cols):
        return pl.BlockSpec((tm, cols), lambda i: (i, 0))

    tab = pl.BlockSpec((tm, SLOT), lambda i: (jnp.where(i < split, i % n0, (i - split) % n1), 0))
    hs = N_HEADS_B * SLOT
    out_cols = (A_Q_COLS, 2 * SLOT, 2 * SLOT, hs, hs, hs)
    return pl.pallas_call(
        _proj_kernel,
        grid=(t // tm,),
        in_specs=[row(D_MODEL), _layer_spec((1, D_MODEL), layer), _const_spec((D_MODEL, W1_COLS)),
                  _layer_spec((1, Q_LORA_RANK), layer), _layer_spec((1, KV_LORA_RANK), layer),
                  _const_spec((Q_LORA_RANK, 2 * hs)), _layer_spec((KV_LORA_RANK, hs), layer),
                  tab, tab, tab],
        out_specs=[row(c) for c in out_cols],
        out_shape=[jax.ShapeDtypeStruct((t, c), BF16) for c in out_cols],
        compiler_params=_params(1),
        name="proj",
    )(x, g, w1, qg, kvg, wq, wkv, *rope_tabs)


def _bias_kernel(rb_ref, bucket_ref, o_ref):
    bucket = bucket_ref[0]
    for h in range(N_HEADS_A):
        acc = jnp.full(bucket.shape, -jnp.inf, F32)
        for b in range(N_BUCKETS):
            acc = jnp.where(bucket == b, rb_ref[b, h], acc)
        o_ref[0, h] = acc


def _bias_table(rel_bias, bucket):
    kw = 3 * BLOCK
    return pl.pallas_call(
        _bias_kernel,
        grid=(3,),
        in_specs=[pl.BlockSpec(memory_space=pltpu.SMEM),
                  pl.BlockSpec((1, BLOCK, kw), lambda e: (e, 0, 0))],
        out_specs=pl.BlockSpec((1, N_HEADS_A, BLOCK, kw), lambda e: (e, 0, 0, 0)),
        out_shape=jax.ShapeDtypeStruct((3, N_HEADS_A, BLOCK, kw), F32),
        compiler_params=_params(1),
        name="t5_bias",
    )(rel_bias, bucket)


def _t5_bucket(rel):
    half = N_BUCKETS // 2
    max_exact = half // 2
    ret = jnp.where(rel > 0, half, 0)
    n = jnp.abs(rel)
    nf = jnp.maximum(n, 1).astype(F32)
    large = max_exact + (jnp.log(nf / max_exact) / math.log(MAX_DISTANCE / max_exact)
                         * (half - max_exact)).astype(jnp.int32)
    large = jnp.minimum(large, half - 1)
    return ret + jnp.where(n < max_exact, n, large)


def _band_buckets():
    qi = jnp.arange(BLOCK, dtype=jnp.int32)[:, None]
    si = jnp.arange(3 * BLOCK, dtype=jnp.int32)[None, :]
    out = []
    for shift in (0, -BLOCK, -2 * BLOCK):
        rel = si + shift - qi
        out.append(jnp.where(jnp.abs(rel) <= WINDOW, _t5_bucket(rel), -1))
    return jnp.stack(out).astype(jnp.int32)


def _win_kernel(q_ref, k_ref, v_ref, bias_ref, sink_ref, o_ref, s0_ref, s1_ref, *, seq):
    step = pl.program_id(1)
    nb = seq // BLOCK
    kw = 3 * BLOCK
    gw = GROUP_A * HEAD_DIM_A
    lane = lax.broadcasted_iota(jnp.int32, (BLOCK, gw), 1)
    keep = [(lane >= r * HEAD_DIM_A) & (lane < (r + 1) * HEAD_DIM_A) for r in range(GROUP_A)]
    s_bufs = (s0_ref, s1_ref)
    row0 = pl.multiple_of(jnp.minimum(step, 0), 8)
    tile_rows = pl.ds(row0, GROUP_A * BLOCK)
    units = [(blk, g) for blk in range(WIN_BLOCKS) for g in range(N_KV_HEADS_A)]
    windows = []
    for blk in range(WIN_BLOCKS):
        n = step * WIN_BLOCKS + blk
        start = pl.multiple_of(jnp.clip((n - 1) * BLOCK, 0, seq - kw), BLOCK)
        windows.append((start, jnp.where(n == 0, 0, jnp.where(n == nb - 1, 2, 1))))

    def scores(u):
        blk, g = units[u]
        start, edge = windows[blk]
        q_grp = q_ref[blk * BLOCK:(blk + 1) * BLOCK, g * gw:(g + 1) * gw]
        q_stack = jnp.concatenate(
            [jnp.where(keep[r], q_grp, jnp.zeros_like(q_grp)) for r in range(GROUP_A)], axis=0)
        k_dup = k_ref[pl.ds(start, kw), g * SLOT:(g + 1) * SLOT]
        k_rep = jnp.concatenate([k_dup, k_dup], axis=1)
        s = lax.dot_general(q_stack, k_rep, (((1,), (1,)), ((), ())),
                            preferred_element_type=F32) + bias_ref[edge, g]
        s_bufs[u % 2][tile_rows, :] = s
        return jnp.max(jnp.concatenate([s, sink_ref[g]], axis=1), axis=-1, keepdims=True)

    def weighted(u, m):
        blk, g = units[u]
        start, _ = windows[blk]
        v_dup = v_ref[pl.ds(start, kw), g * SLOT:(g + 1) * SLOT]
        v_rep = jnp.concatenate([v_dup, v_dup], axis=1)
        p = jnp.exp(s_bufs[u % 2][tile_rows, :] - m)
        denom = (jnp.sum(p, axis=-1, keepdims=True)
                 + jnp.sum(jnp.exp(sink_ref[g] - m), axis=-1, keepdims=True))
        o_all = jnp.dot(p.astype(BF16), v_rep, preferred_element_type=F32) / denom
        out = jnp.zeros((BLOCK, gw), F32)
        for r in range(GROUP_A):
            out = jnp.where(keep[r], o_all[r * BLOCK:(r + 1) * BLOCK], out)
        o_ref[blk * BLOCK:(blk + 1) * BLOCK, g * gw:(g + 1) * gw] = out.astype(BF16)

    m = scores(0)
    for u in range(len(units)):
        m_next = scores(u + 1) if u + 1 < len(units) else None
        weighted(u, m)
        m = m_next


def _stream_block_specs(tq, seq, token0, cols):
    assert token0 % seq == 0 and seq % tq == 0
    nq = seq // tq
    q_spec = pl.BlockSpec((tq, cols), lambda b, i: (token0 // tq + b * nq + i, 0))
    seq_spec = pl.BlockSpec((seq, cols), lambda b, i: (token0 // seq + b, 0))
    return q_spec, seq_spec


def _window_attention(qa, ka, va, bias, sink_rows, batch, seq, token0):
    tq = WIN_BLOCKS * BLOCK
    q_spec, _ = _stream_block_specs(tq, seq, token0, A_Q_COLS)
    _, k_spec = _stream_block_specs(tq, seq, token0, ka.shape[-1])
    _, v_spec = _stream_block_specs(tq, seq, token0, va.shape[-1])
    nq = seq // tq
    return pl.pallas_call(
        functools.partial(_win_kernel, seq=seq),
        grid=(batch, nq),
        in_specs=[q_spec, k_spec, v_spec, _const_spec(bias.shape), _const_spec(sink_rows.shape)],
        out_specs=pl.BlockSpec((tq, A_Q_COLS), lambda b, i: (b * nq + i, 0)),
        out_shape=jax.ShapeDtypeStruct((batch * seq, A_Q_COLS), BF16),
        scratch_shapes=[pltpu.VMEM((GROUP_A * BLOCK, 3 * BLOCK), F32),
                        pltpu.VMEM((GROUP_A * BLOCK, 3 * BLOCK), F32)],
        compiler_params=_params(2),
        name="window_attn",
    )(qa, ka, va, bias, sink_rows)


def _mla_kernel(q_ref, k_ref, v_ref, o_ref, s0_ref, s1_ref, *, seq):
    c2 = (QK_NOPE_DIM + QK_ROPE_DIM) ** -0.5 * math.log2(math.e)
    n_chunks = seq // MLA_KC
    sub = MLA_SUB
    lane = lax.broadcasted_iota(jnp.int32, (sub, SLOT), 1)
    s_bufs = (s0_ref, s1_ref)
    row0 = pl.multiple_of(jnp.minimum(pl.program_id(1), 0), 16)
    tile_rows = pl.ds(row0, sub)
    units = [(t, h) for t in range(q_ref.shape[0] // sub) for h in range(N_HEADS_B)]

    def scores(u):
        t, h = units[u]
        q_h = q_ref[t * sub:(t + 1) * sub, h * SLOT:(h + 1) * SLOT]
        m = jnp.full((sub, 1), -jnp.inf, F32)
        for c in range(n_chunks):
            rows = slice(c * MLA_KC, (c + 1) * MLA_KC)
            s = lax.dot_general(q_h, k_ref[rows, h * SLOT:(h + 1) * SLOT], (((1,), (1,)), ((), ())),
                                preferred_element_type=F32)
            s_bufs[u % 2][tile_rows, rows] = s
            m = jnp.maximum(m, jnp.max(s, axis=-1, keepdims=True))
        return m

    def weighted(u, m):
        _, h = units[u]
        lo = (h // 2) * 2 * SLOT
        acc = jnp.zeros((sub, 2 * SLOT), F32)
        for c in range(n_chunks):
            rows = slice(c * MLA_KC, (c + 1) * MLA_KC)
            p = jnp.exp2((s_bufs[u % 2][tile_rows, rows] - m) * c2)
            acc = acc + jnp.dot(p.astype(BF16), v_ref[rows, lo:lo + 2 * SLOT],
                                preferred_element_type=F32)
        return acc[:, (h % 2) * SLOT:(h % 2 + 1) * SLOT]

    m = scores(0)
    even = None
    for u, (t, h) in enumerate(units):
        m_next = scores(u + 1) if u + 1 < len(units) else None
        acc = weighted(u, m)
        out_h = acc / acc[:, 0:1]
        if h % 2 == 0:
            even = out_h
        else:
            pair = jnp.where(lane < V_DIM_B, pltpu.roll(even, SLOT - V_DIM_B, 1), out_h)
            o_ref[t * sub:(t + 1) * sub, (h // 2) * SLOT:(h // 2 + 1) * SLOT] = pair.astype(BF16)
        m = m_next


def _mla_attention(q, k, v, batch, seq, token0):
    tq = MLA_TQ
    hs = N_HEADS_B * SLOT
    oc = N_HEADS_B * V_DIM_B
    q_spec, seq_spec = _stream_block_specs(tq, seq, token0, hs)
    nq = seq // tq
    return pl.pallas_call(
        functools.partial(_mla_kernel, seq=seq),
        grid=(batch, nq),
        in_specs=[q_spec, seq_spec, seq_spec],
        out_specs=pl.BlockSpec((tq, oc), lambda b, i: (b * nq + i, 0)),
        out_shape=jax.ShapeDtypeStruct((batch * seq, oc), BF16),
        scratch_shapes=[pltpu.VMEM((MLA_SUB, seq), F32), pltpu.VMEM((MLA_SUB, seq), F32)],
        compiler_params=_params(2),
        name="mla_attn",
    )(q, k, v)


def _merge_kernel(x_ref, oa0_ref, oa1_ref, ob0_ref, ob1_ref, pre_ref, post_ref, wga_ref, wgb_ref,
                  wa_ref, wb_ref, wo_ref, o_ref, *, split):
    first = pl.program_id(0) < split
    for sub in range(x_ref.shape[0] // MERGE_SUB):
        rows = slice(sub * MERGE_SUB, (sub + 1) * MERGE_SUB)
        x = x_ref[rows, :]
        hb = _rms(x, pre_ref[...]).astype(BF16)
        oa = _stream_rows((oa0_ref, oa1_ref), rows, first)
        ob = _stream_rows((ob0_ref, ob1_ref), rows, first)
        ga = jax.nn.sigmoid(jnp.dot(hb, wga_ref[...], preferred_element_type=F32))
        merged = ga * jnp.dot(oa, wa_ref[...], preferred_element_type=F32)
        gb = jax.nn.sigmoid(jnp.dot(hb, wgb_ref[...], preferred_element_type=F32))
        merged = merged + gb * jnp.dot(ob, wb_ref[...], preferred_element_type=F32)
        y = jnp.dot(merged.astype(BF16), wo_ref[...], preferred_element_type=F32)
        o_ref[rows, :] = x + _rms(y, post_ref[...])


def _merge(x, oas, obs, layer, pre_g, post_g, wga, wgb, wa, wb, wo, split):
    t = x.shape[0]
    tm = MERGE_TM

    def row(cols):
        return pl.BlockSpec((tm, cols), lambda i: (i, 0))

    bcols = N_HEADS_B * V_DIM_B
    return pl.pallas_call(
        functools.partial(_merge_kernel, split=split),
        grid=(t // tm,),
        in_specs=[row(D_MODEL)] + _stream_specs(tm, A_Q_COLS, split, 2) + _stream_specs(tm, bcols, split, 2)
        + [_layer_spec((1, D_MODEL), layer), _layer_spec((1, D_MODEL), layer),
           _const_spec((D_MODEL, D_MODEL)), _const_spec((D_MODEL, D_MODEL)),
           _layer_spec((A_Q_COLS, D_MODEL), layer), _layer_spec((bcols, D_MODEL), layer),
           _layer_spec((D_MODEL, D_MODEL), layer)],
        out_specs=row(D_MODEL),
        out_shape=jax.ShapeDtypeStruct((t, D_MODEL), F32),
        compiler_params=_params(1),
        name="merge",
    )(x, *oas, *obs, pre_g, post_g, wga, wgb, wa, wb, wo)


def _layer_weights(wi, uq):
    o = 0
    pieces = {}
    for name, sz in (("qa", A_Q_COLS), ("ka", A_KV_COLS), ("va", A_KV_COLS), ("cq", Q_LORA_RANK),
                     ("ckv", KV_LORA_RANK), ("kr", QK_ROPE_DIM), ("ga", D_MODEL), ("gb", D_MODEL)):
        pieces[name] = wi[:, o:o + sz]
        o += sz
    z64 = jnp.zeros((D_MODEL, 64), BF16)
    z32 = jnp.zeros((D_MODEL, 32), BF16)
    w1 = jnp.concatenate([
        pieces["qa"] * (HEAD_DIM_A ** -0.5),
        pieces["ka"], pieces["va"], pieces["cq"], pieces["ckv"],
        z64, pieces["kr"], z32,
    ], axis=1)

    uq = uq.reshape(Q_LORA_RANK, N_HEADS_B, QK_NOPE_DIM + QK_ROPE_DIM)
    q_rope = uq[..., QK_NOPE_DIM:]
    r1, r2 = q_rope[..., :QK_ROPE_DIM // 2], q_rope[..., QK_ROPE_DIM // 2:]
    zq = jnp.zeros((Q_LORA_RANK, N_HEADS_B, SLOT - QK_NOPE_DIM - QK_ROPE_DIM), BF16)
    zn = jnp.zeros((Q_LORA_RANK, N_HEADS_B, QK_NOPE_DIM), BF16)
    q_main = jnp.concatenate([uq, zq], axis=-1).reshape(Q_LORA_RANK, -1)
    q_swap = jnp.concatenate([zn, r2, r1, zq], axis=-1).reshape(Q_LORA_RANK, -1)
    wq = jnp.concatenate([q_main, q_swap], axis=1)
    return w1, wq, pieces["ga"], pieces["gb"]


def _sink_rows(sink_l):
    col = jnp.repeat(sink_l, BLOCK).reshape(N_KV_HEADS_A, GROUP_A * BLOCK, 1)
    lane = jnp.arange(SLOT)[None, None, :]
    return jnp.where(lane == 0, col, -jnp.inf).astype(F32)


def _rope_slot_tables(seq):
    half = QK_ROPE_DIM // 2
    inv_freq = ROPE_THETA ** (-np.arange(0, QK_ROPE_DIM, 2, dtype=np.float64) / QK_ROPE_DIM)
    ang = np.arange(seq, dtype=np.float64)[:, None] * inv_freq[None, :]
    cos, sin = np.cos(ang), np.sin(ang)
    z_half = np.zeros((seq, half))
    z_nope = np.zeros((seq, QK_NOPE_DIM))
    pad = np.zeros((seq, SLOT - QK_NOPE_DIM - QK_ROPE_DIM))
    cos_tab = np.concatenate([np.ones((seq, QK_NOPE_DIM)), cos, cos, pad], axis=1)
    sin_lo = np.concatenate([z_nope, -sin, z_half, pad], axis=1)
    sin_hi = np.concatenate([z_nope, z_half, sin, pad], axis=1)
    return tuple(jnp.asarray(t, dtype=F32) for t in (cos_tab, sin_lo, sin_hi))


def _trunk(x_first, x_second, bias, shared, layers):
    shapes = (x_first.shape, x_second.shape)
    xs = tuple(x.reshape(-1, D_MODEL) for x in (x_first, x_second))
    t0 = xs[0].shape[0]
    assert FFN_TM == PROJ_TM == MERGE_TM and t0 % FFN_TM == 0 and xs[1].shape[0] % FFN_TM == 0
    split = t0 // FFN_TM
    seqs = (shapes[0][1], shapes[1][1])
    streams = ((shapes[0][0], seqs[0], 0), (shapes[1][0], seqs[1], t0))
    rope_tabs = _rope_slot_tables(max(seqs))
    for l, lw in enumerate(layers):
        x = _ffn(xs, l, *shared["ffn1"], split)
        qa, ka, va, q, k, v = _proj(x, l, shared["mix_pre_g"], lw["w1"], shared["q_norm_g"],
                                    shared["kv_norm_g"], lw["wq"], shared["w_ukv"], rope_tabs, seqs, split)
        oas = [_window_attention(qa, ka, va, bias, lw["sink_rows"], *s) for s in streams]
        obs = [_mla_attention(q, k, v, *s) for s in streams]
        x = _merge(x, oas, obs, l, shared["mix_pre_g"], shared["mix_post_g"], lw["wga"], lw["wgb"],
                   shared["w_a_out"], shared["w_b_out"], shared["w_o"], split)
        last = l == len(layers) - 1
        xs = _ffn((x,), l, *shared["ffn2"], split, n_out=2 if last else 1)
        if not last:
            xs = (xs,)
    return tuple(y.reshape(shape) for y, shape in zip(xs, shapes))


def kernel(x_prompt, x_sample, rel_bias, ffn1_pre_g, ffn1_post_g, ffn1_w_gate, ffn1_w_up, ffn1_w_down, mix_pre_g, mix_post_g, w_in, sink, q_norm_g, kv_norm_g, w_uq, w_ukv, w_a_out, w_b_out, w_o, ffn2_pre_g, ffn2_post_g, ffn2_w_gate, ffn2_w_up, ffn2_w_down):
    def gains(g):
        return g.reshape(DEPTH, 1, -1)

    def bf16(w):
        return w.astype(BF16)

    shared = dict(
        ffn1=(gains(ffn1_pre_g), gains(ffn1_post_g), ffn1_w_gate, ffn1_w_up, ffn1_w_down),
        ffn2=(gains(ffn2_pre_g), gains(ffn2_post_g), ffn2_w_gate, ffn2_w_up, ffn2_w_down),
        mix_pre_g=gains(mix_pre_g), mix_post_g=gains(mix_post_g),
        q_norm_g=gains(q_norm_g), kv_norm_g=gains(kv_norm_g),
        w_a_out=bf16(w_a_out), w_b_out=bf16(w_b_out), w_o=bf16(w_o), w_ukv=bf16(w_ukv),
    )
    w_in_b, w_uq_b = bf16(w_in), bf16(w_uq)
    layers = []
    for l in range(DEPTH):
        w1, wq, wga, wgb = _layer_weights(w_in_b[l], w_uq_b[l])
        layers.append(dict(w1=w1, wq=wq, wga=wga, wgb=wgb, sink_rows=_sink_rows(sink[l])))
    bias = _bias_table(rel_bias, _band_buckets())
    bias = bias.reshape(3, N_KV_HEADS_A, GROUP_A * BLOCK, 3 * BLOCK)
    return _trunk(x_prompt, x_sample, bias, shared, layers)
```

```python
import functools
import math

import jax
import jax.numpy as jnp
import numpy as np
from jax import lax
from jax.experimental import pallas as pl
from jax.experimental.pallas import tpu as pltpu

F32 = jnp.float32
BF16 = jnp.bfloat16

D_MODEL = 1024
DEPTH = 2
N_HEADS_A = 8
N_KV_HEADS_A = 2
GROUP_A = N_HEADS_A // N_KV_HEADS_A
HEAD_DIM_A = 64
WINDOW = 128
BLOCK = 128
N_HEADS_B = 8
QK_NOPE_DIM = 64
QK_ROPE_DIM = 32
V_DIM_B = 64
Q_LORA_RANK = 384
KV_LORA_RANK = 256
ROPE_THETA = 10000.0
N_BUCKETS = 32
MAX_DISTANCE = 128
D_FF = 2816
FFN_RES_WEIGHT = 0.5
EPS = 1e-6

A_Q_COLS = N_HEADS_A * HEAD_DIM_A
A_KV_COLS = N_KV_HEADS_A * HEAD_DIM_A

LANES = 128
SLOT = LANES
VMEM_LIMIT_BYTES = 56 * 1024 * 1024

W1_QA = (0, 512)
W1_KA = (512, 640)
W1_VA = (640, 768)
W1_CQ = (768, 1152)
W1_CKV = (1152, 1408)
W1_KR = (1408, 1536)
W1_COLS = 1536

FFN_CHUNKS = ((0, 1024), (1024, 1024), (2048, 768))
FFN_WEIGHT_PIECES = 16
FFN_STAGE_SLOTS = 4
ROW_SUB = 256
MERGE_SUB = 256
PROJ_SUB = 1024
FFN_TM = 1024
PROJ_TM = 1024
MERGE_TM = 1024
MLA_TQ = 512
MLA_SUB = 256
MLA_KC = 1024
WIN_BLOCKS = 16


def _rms(x, g):
    return x * lax.rsqrt(jnp.mean(x * x, axis=-1, keepdims=True) + EPS) * g


def _const_spec(shape):
    nd = len(shape)
    return pl.BlockSpec(shape, lambda *_: (0,) * nd, pipeline_mode=pl.Buffered(1))


def _layer_spec(shape, layer):
    nd = len(shape)
    return pl.BlockSpec((None,) + tuple(shape), lambda *_: (layer,) + (0,) * nd,
                        pipeline_mode=pl.Buffered(1))


def _params(n_axes, flags=None):
    return pltpu.CompilerParams(
        dimension_semantics=("parallel",) * n_axes,
        vmem_limit_bytes=VMEM_LIMIT_BYTES,
        flags=flags,
    )


def _stream_specs(tm, cols, split, n):
    if n == 1:
        return [pl.BlockSpec((tm, cols), lambda i: (i, 0))]
    return [pl.BlockSpec((tm, cols), lambda i: (jnp.minimum(i, split - 1), 0)),
            pl.BlockSpec((tm, cols), lambda i: (jnp.maximum(i - split, 0), 0))]


def _stream_rows(refs, rows, first):
    if len(refs) == 1:
        return refs[0][rows, :]
    return jnp.where(first, refs[0][rows, :], refs[1][rows, :])


def _load_weight_bf16(w_hbm, w_ref, stage_ref, sem_ref):
    n_slots, rows = stage_ref.shape[0], stage_ref.shape[1]
    n_pieces = w_ref.shape[0] // rows

    def piece_copy(k):
        slot = k % n_slots
        return pltpu.make_async_copy(w_hbm.at[pl.ds(k * rows, rows), :], stage_ref.at[slot],
                                     sem_ref.at[slot])

    for k in range(min(n_slots - 1, n_pieces)):
        piece_copy(k).start()
    for k in range(n_pieces):
        if k + n_slots - 1 < n_pieces:
            piece_copy(k + n_slots - 1).start()
        piece_copy(k).wait()
        w_ref[k * rows:(k + 1) * rows, :] = stage_ref[k % n_slots].astype(BF16)


def _ffn_kernel(*refs, n_in, n_out, split, layer):
    x_refs, refs = refs[:n_in], refs[n_in:]
    pre_ref, post_ref, wg_hbm, wu_hbm, wd_hbm = refs[:5]
    o_refs, refs = refs[5:5 + n_out], refs[5 + n_out:]
    acc_ref, wg_ref, wu_ref, wd_ref, stage_in_ref, stage_out_ref, sem_ref = refs
    first = pl.program_id(0) < split

    @pl.when(pl.program_id(0) == 0)
    def _():
        _load_weight_bf16(wg_hbm.at[layer], wg_ref, stage_in_ref, sem_ref)
        _load_weight_bf16(wu_hbm.at[layer], wu_ref, stage_in_ref, sem_ref)
        _load_weight_bf16(wd_hbm.at[layer], wd_ref, stage_out_ref, sem_ref)

    for sub in range(x_refs[0].shape[0] // ROW_SUB):
        rows = slice(sub * ROW_SUB, (sub + 1) * ROW_SUB)
        x = _stream_rows(x_refs, rows, first)
        hb = _rms(x, pre_ref[...]).astype(BF16)
        for idx, (c0, cw) in enumerate(FFN_CHUNKS):
            g = jnp.dot(hb, wg_ref[:, c0:c0 + cw], preferred_element_type=F32)
            u = jnp.dot(hb, wu_ref[:, c0:c0 + cw], preferred_element_type=F32)
            a = (jax.nn.silu(g) * u).astype(BF16)
            part = jnp.dot(a, wd_ref[c0:c0 + cw, :], preferred_element_type=F32)
            if idx == 0:
                acc_ref[rows, :] = part
            else:
                acc_ref[rows, :] += part
        y = x + FFN_RES_WEIGHT * _rms(acc_ref[rows, :], post_ref[...])
        o_refs[-1][rows, :] = y
    if n_out == 2:
        @pl.when(first)
        def _():
            o_refs[0][...] = o_refs[1][...]


def _ffn(xs, layer, pre_g, post_g, wg, wu, wd, split, n_out=1):
    tm = FFN_TM
    t = sum(x.shape[0] for x in xs)
    n_in = len(xs)
    hbm = pl.BlockSpec(memory_space=pl.ANY)
    if n_out == 1:
        out_shape = [jax.ShapeDtypeStruct((t, D_MODEL), F32)]
    else:
        out_shape = [jax.ShapeDtypeStruct((split * tm, D_MODEL), F32),
                     jax.ShapeDtypeStruct((t - split * tm, D_MODEL), F32)]
    outs = pl.pallas_call(
        functools.partial(_ffn_kernel, n_in=n_in, n_out=n_out, split=split, layer=layer),
        grid=(t // tm,),
        in_specs=_stream_specs(tm, D_MODEL, split, n_in)
        + [_layer_spec((1, D_MODEL), layer), _layer_spec((1, D_MODEL), layer), hbm, hbm, hbm],
        out_specs=_stream_specs(tm, D_MODEL, split, n_out),
        out_shape=out_shape,
        scratch_shapes=[pltpu.VMEM((tm, D_MODEL), F32),
                        pltpu.VMEM((D_MODEL, D_FF), BF16), pltpu.VMEM((D_MODEL, D_FF), BF16),
                        pltpu.VMEM((D_FF, D_MODEL), BF16),
                        pltpu.VMEM((FFN_STAGE_SLOTS, D_MODEL // FFN_WEIGHT_PIECES, D_FF), F32),
                        pltpu.VMEM((FFN_STAGE_SLOTS, D_FF // FFN_WEIGHT_PIECES, D_MODEL), F32),
                        pltpu.SemaphoreType.DMA((FFN_STAGE_SLOTS,))],
        compiler_params=pltpu.CompilerParams(dimension_semantics=("arbitrary",),
                                             vmem_limit_bytes=VMEM_LIMIT_BYTES),
        name="ffn",
    )(*xs, pre_g, post_g, wg, wu, wd)
    return outs[0] if n_out == 1 else outs


def _rope(x, cos, sin_lo, sin_hi):
    half = QK_ROPE_DIM // 2
    return x * cos + pltpu.roll(x, SLOT - half, 1) * sin_lo + pltpu.roll(x, half, 1) * sin_hi


def _proj_kernel(x_ref, g_ref, w1_ref, qg_ref, kvg_ref, wq_ref, wkv_ref, cos_ref, sin_lo_ref, sin_hi_ref,
                 qa_ref, ka_ref, va_ref, q_ref, k_ref, v_ref):
    hs = N_HEADS_B * SLOT
    half = SLOT // 2
    lane = lax.broadcasted_iota(jnp.int32, (PROJ_SUB, SLOT), 1)
    low = lane < half
    one_at_zero = jnp.where(lane == 0, 1.0, 0.0)

    def dup_halves(x):
        swapped = pltpu.roll(x, half, 1)
        return jnp.where(low, x, swapped), jnp.where(low, swapped, x)

    for sub in range(x_ref.shape[0] // PROJ_SUB):
        rows = slice(sub * PROJ_SUB, (sub + 1) * PROJ_SUB)
        hb = _rms(x_ref[rows, :], g_ref[...]).astype(BF16)
        proj = jnp.dot(hb, w1_ref[...], preferred_element_type=F32)
        qa_ref[rows, :] = proj[:, W1_QA[0]:W1_QA[1]].astype(BF16)
        for src, dst in ((W1_KA, ka_ref), (W1_VA, va_ref)):
            first, second = dup_halves(proj[:, src[0]:src[1]])
            dst[rows, :SLOT] = first.astype(BF16)
            dst[rows, SLOT:] = second.astype(BF16)
        cqn = _rms(proj[:, W1_CQ[0]:W1_CQ[1]], qg_ref[...]).astype(BF16)
        ckvn = _rms(proj[:, W1_CKV[0]:W1_CKV[1]], kvg_ref[...]).astype(BF16)
        cos = cos_ref[rows, :]
        sin_lo = sin_lo_ref[rows, :]
        sin_hi = sin_hi_ref[rows, :]
        k_rope = _rope(proj[:, W1_KR[0]:W1_KR[1]], cos, sin_lo, sin_hi)
        qf = jnp.dot(cqn, wq_ref[...], preferred_element_type=F32)
        sin_signed = sin_lo + sin_hi
        kvf = jnp.dot(ckvn, wkv_ref[...], preferred_element_type=F32)
        for h in range(N_HEADS_B):
            lo, hi = h * SLOT, (h + 1) * SLOT
            q_h = qf[:, lo:hi] * cos + qf[:, hs + lo:hs + hi] * sin_signed
            q_ref[rows, lo:hi] = q_h.astype(BF16)
            kv_h = kvf[:, lo:hi]
            k_ref[rows, lo:hi] = jnp.where(low, kv_h, k_rope).astype(BF16)
            v_ref[rows, lo:hi] = jnp.where(low, one_at_zero, kv_h).astype(BF16)


def _proj(x, layer, g, w1, qg, kvg, wq, wkv, rope_tabs, seqs, split):
    t = x.shape[0]
    tm = PROJ_TM
    n0, n1 = seqs[0] // tm, seqs[1] // tm

    def row(cols):
        return pl.BlockSpec((tm, cols), lambda i: (i, 0))

    tab = pl.BlockSpec((tm, SLOT), lambda i: (jnp.where(i < split, i % n0, (i - split) % n1), 0))
    hs = N_HEADS_B * SLOT
    out_cols = (A_Q_COLS, 2 * SLOT, 2 * SLOT, hs, hs, hs)
    return pl.pallas_call(
        _proj_kernel,
        grid=(t // tm,),
        in_specs=[row(D_MODEL), _layer_spec((1, D_MODEL), layer), _const_spec((D_MODEL, W1_COLS)),
                  _layer_spec((1, Q_LORA_RANK), layer), _layer_spec((1, KV_LORA_RANK), layer),
                  _const_spec((Q_LORA_RANK, 2 * hs)), _layer_spec((KV_LORA_RANK, hs), layer),
                  tab, tab, tab],
        out_specs=[row(c) for c in out_cols],
        out_shape=[jax.ShapeDtypeStruct((t, c), BF16) for c in out_cols],
        compiler_params=_params(1),
        name="proj",
    )(x, g, w1, qg, kvg, wq, wkv, *rope_tabs)


def _bias_kernel(rb_ref, bucket_ref, o_ref):
    bucket = bucket_ref[0]
    for h in range(N_HEADS_A):
        acc = jnp.full(bucket.shape, -jnp.inf, F32)
        for b in range(N_BUCKETS):
            acc = jnp.where(bucket == b, rb_ref[b, h], acc)
        o_ref[0, h] = acc


def _bias_table(rel_bias, bucket):
    kw = 3 * BLOCK
    return pl.pallas_call(
        _bias_kernel,
        grid=(3,),
        in_specs=[pl.BlockSpec(memory_space=pltpu.SMEM),
                  pl.BlockSpec((1, BLOCK, kw), lambda e: (e, 0, 0))],
        out_specs=pl.BlockSpec((1, N_HEADS_A, BLOCK, kw), lambda e: (e, 0, 0, 0)),
        out_shape=jax.ShapeDtypeStruct((3, N_HEADS_A, BLOCK, kw), F32),
        compiler_params=_params(1),
        name="t5_bias",
    )(rel_bias, bucket)


def _t5_bucket(rel):
    half = N_BUCKETS // 2
    max_exact = half // 2
    ret = jnp.where(rel > 0, half, 0)
    n = jnp.abs(rel)
    nf = jnp.maximum(n, 1).astype(F32)
    large = max_exact + (jnp.log(nf / max_exact) / math.log(MAX_DISTANCE / max_exact)
                         * (half - max_exact)).astype(jnp.int32)
    large = jnp.minimum(large, half - 1)
    return ret + jnp.where(n < max_exact, n, large)


def _band_buckets():
    qi = jnp.arange(BLOCK, dtype=jnp.int32)[:, None]
    si = jnp.arange(3 * BLOCK, dtype=jnp.int32)[None, :]
    out = []
    for shift in (0, -BLOCK, -2 * BLOCK):
        rel = si + shift - qi
        out.append(jnp.where(jnp.abs(rel) <= WINDOW, _t5_bucket(rel), -1))
    return jnp.stack(out).astype(jnp.int32)


def _win_kernel(q_ref, k_ref, v_ref, bias_ref, sink_ref, o_ref, s0_ref, s1_ref, *, seq):
    step = pl.program_id(1)
    nb = seq // BLOCK
    kw = 3 * BLOCK
    gw = GROUP_A * HEAD_DIM_A
    lane = lax.broadcasted_iota(jnp.int32, (BLOCK, gw), 1)
    keep = [(lane >= r * HEAD_DIM_A) & (lane < (r + 1) * HEAD_DIM_A) for r in range(GROUP_A)]
    s_bufs = (s0_ref, s1_ref)
    row0 = pl.multiple_of(jnp.minimum(step, 0), 8)
    tile_rows = pl.ds(row0, GROUP_A * BLOCK)
    units = [(blk, g) for blk in range(WIN_BLOCKS) for g in range(N_KV_HEADS_A)]
    windows = []
    for blk in range(WIN_BLOCKS):
        n = step * WIN_BLOCKS + blk
        start = pl.multiple_of(jnp.clip((n - 1) * BLOCK, 0, seq - kw), BLOCK)
        windows.append((start, jnp.where(n == 0, 0, jnp.where(n == nb - 1, 2, 1))))

    def scores(u):
        blk, g = units[u]
        start, edge = windows[blk]
        q_grp = q_ref[blk * BLOCK:(blk + 1) * BLOCK, g * gw:(g + 1) * gw]
        q_stack = jnp.concatenate(
            [jnp.where(keep[r], q_grp, jnp.zeros_like(q_grp)) for r in range(GROUP_A)], axis=0)
        k_dup = k_ref[pl.ds(start, kw), g * SLOT:(g + 1) * SLOT]
        k_rep = jnp.concatenate([k_dup, k_dup], axis=1)
        s = lax.dot_general(q_stack, k_rep, (((1,), (1,)), ((), ())),
                            preferred_element_type=F32) + bias_ref[edge, g]
        s_bufs[u % 2][tile_rows, :] = s
        return jnp.max(jnp.concatenate([s, sink_ref[g]], axis=1), axis=-1, keepdims=True)

    def weighted(u, m):
        blk, g = units[u]
        start, _ = windows[blk]
        v_dup = v_ref[pl.ds(start, kw), g * SLOT:(g + 1) * SLOT]
        v_rep = jnp.concatenate([v_dup, v_dup], axis=1)
        p = jnp.exp(s_bufs[u % 2][tile_rows, :] - m)
        denom = (jnp.sum(p, axis=-1, keepdims=True)
                 + jnp.sum(jnp.exp(sink_ref[g] - m), axis=-1, keepdims=True))
        o_all = jnp.dot(p.astype(BF16), v_rep, preferred_element_type=F32) / denom
        out = jnp.zeros((BLOCK, gw), F32)
        for r in range(GROUP_A):
            out = jnp.where(keep[r], o_all[r * BLOCK:(r + 1) * BLOCK], out)
        o_ref[blk * BLOCK:(blk + 1) * BLOCK, g * gw:(g + 1) * gw] = out.astype(BF16)

    m = scores(0)
    for u in range(len(units)):
        m_next = scores(u + 1) if u + 1 < len(units) else None
        weighted(u, m)
        m = m_next


def _stream_block_specs(tq, seq, token0, cols):
    assert token0 % seq == 0 and seq % tq == 0
    nq = seq // tq
    q_spec = pl.BlockSpec((tq, cols), lambda b, i: (token0 // tq + b * nq + i, 0))
    seq_spec = pl.BlockSpec((seq, cols), lambda b, i: (token0 // seq + b, 0))
    return q_spec, seq_spec


def _window_attention(qa, ka, va, bias, sink_rows, batch, seq, token0):
    tq = WIN_BLOCKS * BLOCK
    q_spec, _ = _stream_block_specs(tq, seq, token0, A_Q_COLS)
    _, k_spec = _stream_block_specs(tq, seq, token0, ka.shape[-1])
    _, v_spec = _stream_block_specs(tq, seq, token0, va.shape[-1])
    nq = seq // tq
    return pl.pallas_call(
        functools.partial(_win_kernel, seq=seq),
        grid=(batch, nq),
        in_specs=[q_spec, k_spec, v_spec, _const_spec(bias.shape), _const_spec(sink_rows.shape)],
        out_specs=pl.BlockSpec((tq, A_Q_COLS), lambda b, i: (b * nq + i, 0)),
        out_shape=jax.ShapeDtypeStruct((batch * seq, A_Q_COLS), BF16),
        scratch_shapes=[pltpu.VMEM((GROUP_A * BLOCK, 3 * BLOCK), F32),
                        pltpu.VMEM((GROUP_A * BLOCK, 3 * BLOCK), F32)],
        compiler_params=_params(2),
        name="window_attn",
    )(qa, ka, va, bias, sink_rows)


def _mla_kernel(q_ref, k_ref, v_ref, o_ref, s0_ref, s1_ref, *, seq):
    c2 = (QK_NOPE_DIM + QK_ROPE_DIM) ** -0.5 * math.log2(math.e)
    n_chunks = seq // MLA_KC
    sub = MLA_SUB
    lane = lax.broadcasted_iota(jnp.int32, (sub, SLOT), 1)
    s_bufs = (s0_ref, s1_ref)
    row0 = pl.multiple_of(jnp.minimum(pl.program_id(1), 0), 16)
    tile_rows = pl.ds(row0, sub)
    units = [(t, h) for t in range(q_ref.shape[0] // sub) for h in range(N_HEADS_B)]

    def scores(u):
        t, h = units[u]
        q_h = q_ref[t * sub:(t + 1) * sub, h * SLOT:(h + 1) * SLOT]
        m = jnp.full((sub, 1), -jnp.inf, F32)
        for c in range(n_chunks):
            rows = slice(c * MLA_KC, (c + 1) * MLA_KC)
            s = lax.dot_general(q_h, k_ref[rows, h * SLOT:(h + 1) * SLOT], (((1,), (1,)), ((), ())),
                                preferred_element_type=F32)
            s_bufs[u % 2][tile_rows, rows] = s
            m = jnp.maximum(m, jnp.max(s, axis=-1, keepdims=True))
        return m

    def weighted(u, m):
        _, h = units[u]
        lo = (h // 2) * 2 * SLOT
        acc = jnp.zeros((sub, 2 * SLOT), F32)
        for c in range(n_chunks):
            rows = slice(c * MLA_KC, (c + 1) * MLA_KC)
            p = jnp.exp2((s_bufs[u % 2][tile_rows, rows] - m) * c2)
            acc = acc + jnp.dot(p.astype(BF16), v_ref[rows, lo:lo + 2 * SLOT],
                                preferred_element_type=F32)
        return acc[:, (h % 2) * SLOT:(h % 2 + 1) * SLOT]

    m = scores(0)
    even = None
    for u, (t, h) in enumerate(units):
        m_next = scores(u + 1) if u + 1 < len(units) else None
        acc = weighted(u, m)
        out_h = acc / acc[:, 0:1]
        if h % 2 == 0:
            even = out_h
        else:
            pair = jnp.where(lane < V_DIM_B, pltpu.roll(even, SLOT - V_DIM_B, 1), out_h)
            o_ref[t * sub:(t + 1) * sub, (h // 2) * SLOT:(h // 2 + 1) * SLOT] = pair.astype(BF16)
        m = m_next


def _mla_attention(q, k, v, batch, seq, token0):
    tq = MLA_TQ
    hs = N_HEADS_B * SLOT
    oc = N_HEADS_B * V_DIM_B
    q_spec, seq_spec = _stream_block_specs(tq, seq, token0, hs)
    nq = seq // tq
    return pl.pallas_call(
        functools.partial(_mla_kernel, seq=seq),
        grid=(batch, nq),
        in_specs=[q_spec, seq_spec, seq_spec],
        out_specs=pl.BlockSpec((tq, oc), lambda b, i: (b * nq + i, 0)),
        out_shape=jax.ShapeDtypeStruct((batch * seq, oc), BF16),
        scratch_shapes=[pltpu.VMEM((MLA_SUB, seq), F32), pltpu.VMEM((MLA_SUB, seq), F32)],
        compiler_params=_params(2),
        name="mla_attn",
    )(q, k, v)


def _merge_kernel(x_ref, oa0_ref, oa1_ref, ob0_ref, ob1_ref, pre_ref, post_ref, wga_ref, wgb_ref,
                  wa_ref, wb_ref, wo_ref, o_ref, *, split):
    first = pl.program_id(0) < split
    for sub in range(x_ref.shape[0] // MERGE_SUB):
        rows = slice(sub * MERGE_SUB, (sub + 1) * MERGE_SUB)
        x = x_ref[rows, :]
        hb = _rms(x, pre_ref[...]).astype(BF16)
        oa = _stream_rows((oa0_ref, oa1_ref), rows, first)
        ob = _stream_rows((ob0_ref, ob1_ref), rows, first)
        ga = jax.nn.sigmoid(jnp.dot(hb, wga_ref[...], preferred_element_type=F32))
        merged = ga * jnp.dot(oa, wa_ref[...], preferred_element_type=F32)
        gb = jax.nn.sigmoid(jnp.dot(hb, wgb_ref[...], preferred_element_type=F32))
        merged = merged + gb * jnp.dot(ob, wb_ref[...], preferred_element_type=F32)
        y = jnp.dot(merged.astype(BF16), wo_ref[...], preferred_element_type=F32)
        o_ref[rows, :] = x + _rms(y, post_ref[...])


def _merge(x, oas, obs, layer, pre_g, post_g, wga, wgb, wa, wb, wo, split):
    t = x.shape[0]
    tm = MERGE_TM

    def row(cols):
        return pl.BlockSpec((tm, cols), lambda i: (i, 0))

    bcols = N_HEADS_B * V_DIM_B
    return pl.pallas_call(
        functools.partial(_merge_kernel, split=split),
        grid=(t // tm,),
        in_specs=[row(D_MODEL)] + _stream_specs(tm, A_Q_COLS, split, 2) + _stream_specs(tm, bcols, split, 2)
        + [_layer_spec((1, D_MODEL), layer), _layer_spec((1, D_MODEL), layer),
           _const_spec((D_MODEL, D_MODEL)), _const_spec((D_MODEL, D_MODEL)),
           _layer_spec((A_Q_COLS, D_MODEL), layer), _layer_spec((bcols, D_MODEL), layer),
           _layer_spec((D_MODEL, D_MODEL), layer)],
        out_specs=row(D_MODEL),
        out_shape=jax.ShapeDtypeStruct((t, D_MODEL), F32),
        compiler_params=_params(1),
        name="merge",
    )(x, *oas, *obs, pre_g, post_g, wga, wgb, wa, wb, wo)


def _layer_weights(wi, uq):
    o = 0
    pieces = {}
    for name, sz in (("qa", A_Q_COLS), ("ka", A_KV_COLS), ("va", A_KV_COLS), ("cq", Q_LORA_RANK),
                     ("ckv", KV_LORA_RANK), ("kr", QK_ROPE_DIM), ("ga", D_MODEL), ("gb", D_MODEL)):
        pieces[name] = wi[:, o:o + sz]
        o += sz
    z64 = jnp.zeros((D_MODEL, 64), BF16)
    z32 = jnp.zeros((D_MODEL, 32), BF16)
    w1 = jnp.concatenate([
        pieces["qa"] * (HEAD_DIM_A ** -0.5),
        pieces["ka"], pieces["va"], pieces["cq"], pieces["ckv"],
        z64, pieces["kr"], z32,
    ], axis=1)

    uq = uq.reshape(Q_LORA_RANK, N_HEADS_B, QK_NOPE_DIM + QK_ROPE_DIM)
    q_rope = uq[..., QK_NOPE_DIM:]
    r1, r2 = q_rope[..., :QK_ROPE_DIM // 2], q_rope[..., QK_ROPE_DIM // 2:]
    zq = jnp.zeros((Q_LORA_RANK, N_HEADS_B, SLOT - QK_NOPE_DIM - QK_ROPE_DIM), BF16)
    zn = jnp.zeros((Q_LORA_RANK, N_HEADS_B, QK_NOPE_DIM), BF16)
    q_main = jnp.concatenate([uq, zq], axis=-1).reshape(Q_LORA_RANK, -1)
    q_swap = jnp.concatenate([zn, r2, r1, zq], axis=-1).reshape(Q_LORA_RANK, -1)
    wq = jnp.concatenate([q_main, q_swap], axis=1)
    return w1, wq, pieces["ga"], pieces["gb"]


def _sink_rows(sink_l):
    col = jnp.repeat(sink_l, BLOCK).reshape(N_KV_HEADS_A, GROUP_A * BLOCK, 1)
    lane = jnp.arange(SLOT)[None, None, :]
    return jnp.where(lane == 0, col, -jnp.inf).astype(F32)


def _rope_slot_tables(seq):
    half = QK_ROPE_DIM // 2
    inv_freq = ROPE_THETA ** (-np.arange(0, QK_ROPE_DIM, 2, dtype=np.float64) / QK_ROPE_DIM)
    ang = np.arange(seq, dtype=np.float64)[:, None] * inv_freq[None, :]
    cos, sin = np.cos(ang), np.sin(ang)
    z_half = np.zeros((seq, half))
    z_nope = np.zeros((seq, QK_NOPE_DIM))
    pad = np.zeros((seq, SLOT - QK_NOPE_DIM - QK_ROPE_DIM))
    cos_tab = np.concatenate([np.ones((seq, QK_NOPE_DIM)), cos, cos, pad], axis=1)
    sin_lo = np.concatenate([z_nope, -sin, z_half, pad], axis=1)
    sin_hi = np.concatenate([z_nope, z_half, sin, pad], axis=1)
    return tuple(jnp.asarray(t, dtype=F32) for t in (cos_tab, sin_lo, sin_hi))


def _trunk(x_first, x_second, bias, shared, layers):
    shapes = (x_first.shape, x_second.shape)
    xs = tuple(x.reshape(-1, D_MODEL) for x in (x_first, x_second))
    t0 = xs[0].shape[0]
    assert FFN_TM == PROJ_TM == MERGE_TM and t0 % FFN_TM == 0 and xs[1].shape[0] % FFN_TM == 0
    split = t0 // FFN_TM
    seqs = (shapes[0][1], shapes[1][1])
    streams = ((shapes[0][0], seqs[0], 0), (shapes[1][0], seqs[1], t0))
    rope_tabs = _rope_slot_tables(max(seqs))
    for l, lw in enumerate(layers):
        x = _ffn(xs, l, *shared["ffn1"], split)
        qa, ka, va, q, k, v = _proj(x, l, shared["mix_pre_g"], lw["w1"], shared["q_norm_g"],
                                    shared["kv_norm_g"], lw["wq"], shared["w_ukv"], rope_tabs, seqs, split)
        oas = [_window_attention(qa, ka, va, bias, lw["sink_rows"], *s) for s in streams]
        obs = [_mla_attention(q, k, v, *s) for s in streams]
        x = _merge(x, oas, obs, l, shared["mix_pre_g"], shared["mix_post_g"], lw["wga"], lw["wgb"],
                   shared["w_a_out"], shared["w_b_out"], shared["w_o"], split)
        last = l == len(layers) - 1
        xs = _ffn((x,), l, *shared["ffn2"], split, n_out=2 if last else 1)
        if not last:
            xs = (xs,)
    return tuple(y.reshape(shape) for y, shape in zip(xs, shapes))


def kernel(x_prompt, x_sample, rel_bias, ffn1_pre_g, ffn1_post_g, ffn1_w_gate, ffn1_w_up, ffn1_w_down, mix_pre_g, mix_post_g, w_in, sink, q_norm_g, kv_norm_g, w_uq, w_ukv, w_a_out, w_b_out, w_o, ffn2_pre_g, ffn2_post_g, ffn2_w_gate, ffn2_w_up, ffn2_w_down):
    def gains(g):
        return g.reshape(DEPTH, 1, -1)

    def bf16(w):
        return w.astype(BF16)

    shared = dict(
        ffn1=(gains(ffn1_pre_g), gains(ffn1_post_g), ffn1_w_gate, ffn1_w_up, ffn1_w_down),
        ffn2=(gains(ffn2_pre_g), gains(ffn2_post_g), ffn2_w_gate, ffn2_w_up, ffn2_w_down),
        mix_pre_g=gains(mix_pre_g), mix_post_g=gains(mix_post_g),
        q_norm_g=gains(q_norm_g), kv_norm_g=gains(kv_norm_g),
        w_a_out=bf16(w_a_out), w_b_out=bf16(w_b_out), w_o=bf16(w_o), w_ukv=bf16(w_ukv),
    )
    w_in_b, w_uq_b = bf16(w_in), bf16(w_uq)
    layers = []
    for l in range(DEPTH):
        w1, wq, wga, wgb = _layer_weights(w_in_b[l], w_uq_b[l])
        layers.append(dict(w1=w1, wq=wq, wga=wga, wgb=wgb, sink_rows=_sink_rows(sink[l])))
    bias = _bias_table(rel_bias, _band_buckets())
    bias = bias.reshape(3, N_KV_HEADS_A, GROUP_A * BLOCK, 3 * BLOCK)
    return _trunk(x_prompt, x_sample, bias, shared, layers)
```

```python
import functools
import math

import jax
import jax.numpy as jnp
import numpy as np
from jax import lax
from jax.experimental import pallas as pl
from jax.experimental.pallas import tpu as pltpu

F32 = jnp.float32
BF16 = jnp.bfloat16

D_MODEL = 1024
DEPTH = 2
N_HEADS_A = 8
N_KV_HEADS_A = 2
GROUP_A = N_HEADS_A // N_KV_HEADS_A
HEAD_DIM_A = 64
WINDOW = 128
BLOCK = 128
N_HEADS_B = 8
QK_NOPE_DIM = 64
QK_ROPE_DIM = 32
V_DIM_B = 64
Q_LORA_RANK = 384
KV_LORA_RANK = 256
ROPE_THETA = 10000.0
N_BUCKETS = 32
MAX_DISTANCE = 128
D_FF = 2816
FFN_RES_WEIGHT = 0.5
EPS = 1e-6

A_Q_COLS = N_HEADS_A * HEAD_DIM_A
A_KV_COLS = N_KV_HEADS_A * HEAD_DIM_A

LANES = 128
SLOT = LANES
VMEM_LIMIT_BYTES = 56 * 1024 * 1024

W1_QA = (0, 512)
W1_KA = (512, 640)
W1_VA = (640, 768)
W1_CQ = (768, 1152)
W1_CKV = (1152, 1408)
W1_KR = (1408, 1536)
W1_COLS = 1536

FFN_CHUNKS = ((0, 1024), (1024, 1024), (2048, 768))
FFN_WEIGHT_PIECES = 16
FFN_STAGE_SLOTS = 4
ROW_SUB = 256
MERGE_SUB = 256
PROJ_SUB = 1024
FFN_TM = 1024
PROJ_TM = 1024
MERGE_TM = 1024
MLA_TQ = 512
MLA_SUB = 256
MLA_KC = 1024
WIN_BLOCKS = 16


def _rms(x, g):
    return x * lax.rsqrt(jnp.mean(x * x, axis=-1, keepdims=True) + EPS) * g


def _const_spec(shape):
    nd = len(shape)
    return pl.BlockSpec(shape, lambda *_: (0,) * nd, pipeline_mode=pl.Buffered(1))


def _layer_spec(shape, layer):
    nd = len(shape)
    return pl.BlockSpec((None,) + tuple(shape), lambda *_: (layer,) + (0,) * nd,
                        pipeline_mode=pl.Buffered(1))


def _params(n_axes, flags=None):
    return pltpu.CompilerParams(
        dimension_semantics=("parallel",) * n_axes,
        vmem_limit_bytes=VMEM_LIMIT_BYTES,
        flags=flags,
    )


def _stream_specs(tm, cols, split, n):
    if n == 1:
        return [pl.BlockSpec((tm, cols), lambda i: (i, 0))]
    return [pl.BlockSpec((tm, cols), lambda i: (jnp.minimum(i, split - 1), 0)),
            pl.BlockSpec((tm, cols), lambda i: (jnp.maximum(i - split, 0), 0))]


def _stream_rows(refs, rows, first):
    if len(refs) == 1:
        return refs[0][rows, :]
    return jnp.where(first, refs[0][rows, :], refs[1][rows, :])


def _load_weight_bf16(w_hbm, w_ref, stage_ref, sem_ref):
    n_slots, rows = stage_ref.shape[0], stage_ref.shape[1]
    n_pieces = w_ref.shape[0] // rows

    def piece_copy(k):
        slot = k % n_slots
        return pltpu.make_async_copy(w_hbm.at[pl.ds(k * rows, rows), :], stage_ref.at[slot],
                                     sem_ref.at[slot])

    for k in range(min(n_slots - 1, n_pieces)):
        piece_copy(k).start(priority=k % 2)
    for k in range(n_pieces):
        if k + n_slots - 1 < n_pieces:
            piece_copy(k + n_slots - 1).start(priority=(k + n_slots - 1) % 2)
        piece_copy(k).wait()
        w_ref[k * rows:(k + 1) * rows, :] = stage_ref[k % n_slots].astype(BF16)


def _ffn_kernel(*refs, n_in, n_out, split, layer):
    x_refs, refs = refs[:n_in], refs[n_in:]
    pre_ref, post_ref, wg_hbm, wu_hbm, wd_hbm = refs[:5]
    o_refs, refs = refs[5:5 + n_out], refs[5 + n_out:]
    acc_ref, wg_ref, wu_ref, wd_ref, stage_in_ref, stage_out_ref, sem_ref = refs
    first = pl.program_id(0) < split

    @pl.when(pl.program_id(0) == 0)
    def _():
        _load_weight_bf16(wg_hbm.at[layer], wg_ref, stage_in_ref, sem_ref)
        _load_weight_bf16(wu_hbm.at[layer], wu_ref, stage_in_ref, sem_ref)
        _load_weight_bf16(wd_hbm.at[layer], wd_ref, stage_out_ref, sem_ref)

    for sub in range(x_refs[0].shape[0] // ROW_SUB):
        rows = slice(sub * ROW_SUB, (sub + 1) * ROW_SUB)
        x = _stream_rows(x_refs, rows, first)
        hb = _rms(x, pre_ref[...]).astype(BF16)
        for idx, (c0, cw) in enumerate(FFN_CHUNKS):
            g = jnp.dot(hb, wg_ref[:, c0:c0 + cw], preferred_element_type=F32)
            u = jnp.dot(hb, wu_ref[:, c0:c0 + cw], preferred_element_type=F32)
            a = (jax.nn.silu(g) * u).astype(BF16)
            part = jnp.dot(a, wd_ref[c0:c0 + cw, :], preferred_element_type=F32)
            if idx == 0:
                acc_ref[rows, :] = part
            else:
                acc_ref[rows, :] += part
        y = x + FFN_RES_WEIGHT * _rms(acc_ref[rows, :], post_ref[...])
        o_refs[-1][rows, :] = y
    if n_out == 2:
        @pl.when(first)
        def _():
            o_refs[0][...] = o_refs[1][...]


def _ffn(xs, layer, pre_g, post_g, wg, wu, wd, split, n_out=1):
    tm = FFN_TM
    t = sum(x.shape[0] for x in xs)
    n_in = len(xs)
    hbm = pl.BlockSpec(memory_space=pl.ANY)
    if n_out == 1:
        out_shape = [jax.ShapeDtypeStruct((t, D_MODEL), F32)]
    else:
        out_shape = [jax.ShapeDtypeStruct((split * tm, D_MODEL), F32),
                     jax.ShapeDtypeStruct((t - split * tm, D_MODEL), F32)]
    outs = pl.pallas_call(
        functools.partial(_ffn_kernel, n_in=n_in, n_out=n_out, split=split, layer=layer),
        grid=(t // tm,),
        in_specs=_stream_specs(tm, D_MODEL, split, n_in)
        + [_layer_spec((1, D_MODEL), layer), _layer_spec((1, D_MODEL), layer), hbm, hbm, hbm],
        out_specs=_stream_specs(tm, D_MODEL, split, n_out),
        out_shape=out_shape,
        scratch_shapes=[pltpu.VMEM((tm, D_MODEL), F32),
                        pltpu.VMEM((D_MODEL, D_FF), BF16), pltpu.VMEM((D_MODEL, D_FF), BF16),
                        pltpu.VMEM((D_FF, D_MODEL), BF16),
                        pltpu.VMEM((FFN_STAGE_SLOTS, D_MODEL // FFN_WEIGHT_PIECES, D_FF), F32),
                        pltpu.VMEM((FFN_STAGE_SLOTS, D_FF // FFN_WEIGHT_PIECES, D_MODEL), F32),
                        pltpu.SemaphoreType.DMA((FFN_STAGE_SLOTS,))],
        compiler_params=pltpu.CompilerParams(dimension_semantics=("arbitrary",),
                                             vmem_limit_bytes=VMEM_LIMIT_BYTES),
        name="ffn",
    )(*xs, pre_g, post_g, wg, wu, wd)
    return outs[0] if n_out == 1 else outs


def _rope(x, cos, sin_lo, sin_hi):
    half = QK_ROPE_DIM // 2
    return x * cos + pltpu.roll(x, SLOT - half, 1) * sin_lo + pltpu.roll(x, half, 1) * sin_hi


def _proj_kernel(x_ref, g_ref, w1_ref, qg_ref, kvg_ref, wq_ref, wkv_ref, cos_ref, sin_lo_ref, sin_hi_ref,
                 qa_ref, ka_ref, va_ref, q_ref, k_ref, v_ref):
    hs = N_HEADS_B * SLOT
    half = SLOT // 2
    lane = lax.broadcasted_iota(jnp.int32, (PROJ_SUB, SLOT), 1)
    low = lane < half
    one_at_zero = jnp.where(lane == 0, 1.0, 0.0)

    def dup_halves(x):
        swapped = pltpu.roll(x, half, 1)
        return jnp.where(low, x, swapped), jnp.where(low, swapped, x)

    for sub in range(x_ref.shape[0] // PROJ_SUB):
        rows = slice(sub * PROJ_SUB, (sub + 1) * PROJ_SUB)
        hb = _rms(x_ref[rows, :], g_ref[...]).astype(BF16)
        proj = jnp.dot(hb, w1_ref[...], preferred_element_type=F32)
        qa_ref[rows, :] = proj[:, W1_QA[0]:W1_QA[1]].astype(BF16)
        for src, dst in ((W1_KA, ka_ref), (W1_VA, va_ref)):
            first, second = dup_halves(proj[:, src[0]:src[1]])
            dst[rows, :SLOT] = first.astype(BF16)
            dst[rows, SLOT:] = second.astype(BF16)
        cqn = _rms(proj[:, W1_CQ[0]:W1_CQ[1]], qg_ref[...]).astype(BF16)
        ckvn = _rms(proj[:, W1_CKV[0]:W1_CKV[1]], kvg_ref[...]).astype(BF16)
        cos = cos_ref[rows, :]
        sin_lo = sin_lo_ref[rows, :]
        sin_hi = sin_hi_ref[rows, :]
        k_rope = _rope(proj[:, W1_KR[0]:W1_KR[1]], cos, sin_lo, sin_hi)
        qf = jnp.dot(cqn, wq_ref[...], preferred_element_type=F32)
        sin_signed = sin_lo + sin_hi
        kvf = jnp.dot(ckvn, wkv_ref[...], preferred_element_type=F32)
        for h in range(N_HEADS_B):
            lo, hi = h * SLOT, (h + 1) * SLOT
            q_h = qf[:, lo:hi] * cos + qf[:, hs + lo:hs + hi] * sin_signed
            q_ref[rows, lo:hi] = q_h.astype(BF16)
            kv_h = kvf[:, lo:hi]
            k_ref[rows, lo:hi] = jnp.where(low, kv_h, k_rope).astype(BF16)
            v_ref[rows, lo:hi] = jnp.where(low, one_at_zero, kv_h).astype(BF16)


def _proj(x, layer, g, w1, qg, kvg, wq, wkv, rope_tabs, seqs, split):
    t = x.shape[0]
    tm = PROJ_TM
    n0, n1 = seqs[0] // tm, seqs[1] // tm

    def row(cols):
        return pl.BlockSpec((tm, cols), lambda i: (i, 0))

    tab = pl.BlockSpec((tm, SLOT), lambda i: (jnp.where(i < split, i % n0, (i - split) % n1), 0))
    hs = N_HEADS_B * SLOT
    out_cols = (A_Q_COLS, 2 * SLOT, 2 * SLOT, hs, hs, hs)
    return pl.pallas_call(
        _proj_kernel,
        grid=(t // tm,),
        in_specs=[row(D_MODEL), _layer_spec((1, D_MODEL), layer), _const_spec((D_MODEL, W1_COLS)),
                  _layer_spec((1, Q_LORA_RANK), layer), _layer_spec((1, KV_LORA_RANK), layer),
                  _const_spec((Q_LORA_RANK, 2 * hs)), _layer_spec((KV_LORA_RANK, hs), layer),
                  tab, tab, tab],
        out_specs=[row(c) for c in out_cols],
        out_shape=[jax.ShapeDtypeStruct((t, c), BF16) for c in out_cols],
        compiler_params=_params(1),
        name="proj",
    )(x, g, w1, qg, kvg, wq, wkv, *rope_tabs)


def _bias_kernel(rb_ref, bucket_ref, o_ref):
    bucket = bucket_ref[0]
    for h in range(N_HEADS_A):
        acc = jnp.full(bucket.shape, -jnp.inf, F32)
        for b in range(N_BUCKETS):
            acc = jnp.where(bucket == b, rb_ref[b, h], acc)
        o_ref[0, h] = acc


def _bias_table(rel_bias, bucket):
    kw = 3 * BLOCK
    return pl.pallas_call(
        _bias_kernel,
        grid=(3,),
        in_specs=[pl.BlockSpec(memory_space=pltpu.SMEM),
                  pl.BlockSpec((1, BLOCK, kw), lambda e: (e, 0, 0))],
        out_specs=pl.BlockSpec((1, N_HEADS_A, BLOCK, kw), lambda e: (e, 0, 0, 0)),
        out_shape=jax.ShapeDtypeStruct((3, N_HEADS_A, BLOCK, kw), F32),
        compiler_params=_params(1),
        name="t5_bias",
    )(rel_bias, bucket)


def _t5_bucket(rel):
    half = N_BUCKETS // 2
    max_exact = half // 2
    ret = jnp.where(rel > 0, half, 0)
    n = jnp.abs(rel)
    nf = jnp.maximum(n, 1).astype(F32)
    large = max_exact + (jnp.log(nf / max_exact) / math.log(MAX_DISTANCE / max_exact)
                         * (half - max_exact)).astype(jnp.int32)
    large = jnp.minimum(large, half - 1)
    return ret + jnp.where(n < max_exact, n, large)


def _band_buckets():
    qi = jnp.arange(BLOCK, dtype=jnp.int32)[:, None]
    si = jnp.arange(3 * BLOCK, dtype=jnp.int32)[None, :]
    out = []
    for shift in (0, -BLOCK, -2 * BLOCK):
        rel = si + shift - qi
        out.append(jnp.where(jnp.abs(rel) <= WINDOW, _t5_bucket(rel), -1))
    return jnp.stack(out).astype(jnp.int32)


def _win_kernel(q_ref, k_ref, v_ref, bias_ref, sink_ref, o_ref, s0_ref, s1_ref, *, seq):
    step = pl.program_id(1)
    nb = seq // BLOCK
    kw = 3 * BLOCK
    gw = GROUP_A * HEAD_DIM_A
    lane = lax.broadcasted_iota(jnp.int32, (BLOCK, gw), 1)
    keep = [(lane >= r * HEAD_DIM_A) & (lane < (r + 1) * HEAD_DIM_A) for r in range(GROUP_A)]
    s_bufs = (s0_ref, s1_ref)
    row0 = pl.multiple_of(jnp.minimum(step, 0), 8)
    tile_rows = pl.ds(row0, GROUP_A * BLOCK)
    units = [(blk, g) for blk in range(WIN_BLOCKS) for g in range(N_KV_HEADS_A)]
    windows = []
    for blk in range(WIN_BLOCKS):
        n = step * WIN_BLOCKS + blk
        start = pl.multiple_of(jnp.clip((n - 1) * BLOCK, 0, seq - kw), BLOCK)
        windows.append((start, jnp.where(n == 0, 0, jnp.where(n == nb - 1, 2, 1))))

    def scores(u):
        blk, g = units[u]
        start, edge = windows[blk]
        q_grp = q_ref[blk * BLOCK:(blk + 1) * BLOCK, g * gw:(g + 1) * gw]
        q_stack = jnp.concatenate(
            [jnp.where(keep[r], q_grp, jnp.zeros_like(q_grp)) for r in range(GROUP_A)], axis=0)
        k_dup = k_ref[pl.ds(start, kw), g * SLOT:(g + 1) * SLOT]
        k_rep = jnp.concatenate([k_dup, k_dup], axis=1)
        s = lax.dot_general(q_stack, k_rep, (((1,), (1,)), ((), ())),
                            preferred_element_type=F32) + bias_ref[edge, g]
        s_bufs[u % 2][tile_rows, :] = s
        return jnp.max(jnp.concatenate([s, sink_ref[g]], axis=1), axis=-1, keepdims=True)

    def weighted(u, m):
        blk, g = units[u]
        start, _ = windows[blk]
        v_dup = v_ref[pl.ds(start, kw), g * SLOT:(g + 1) * SLOT]
        v_rep = jnp.concatenate([v_dup, v_dup], axis=1)
        p = jnp.exp(s_bufs[u % 2][tile_rows, :] - m)
        denom = (jnp.sum(p, axis=-1, keepdims=True)
                 + jnp.sum(jnp.exp(sink_ref[g] - m), axis=-1, keepdims=True))
        o_all = jnp.dot(p.astype(BF16), v_rep, preferred_element_type=F32) / denom
        out = jnp.zeros((BLOCK, gw), F32)
        for r in range(GROUP_A):
            out = jnp.where(keep[r], o_all[r * BLOCK:(r + 1) * BLOCK], out)
        o_ref[blk * BLOCK:(blk + 1) * BLOCK, g * gw:(g + 1) * gw] = out.astype(BF16)

    m = scores(0)
    for u in range(len(units)):
        m_next = scores(u + 1) if u + 1 < len(units) else None
        weighted(u, m)
        m = m_next


def _stream_block_specs(tq, seq, token0, cols):
    assert token0 % seq == 0 and seq % tq == 0
    nq = seq // tq
    q_spec = pl.BlockSpec((tq, cols), lambda b, i: (token0 // tq + b * nq + i, 0))
    seq_spec = pl.BlockSpec((seq, cols), lambda b, i: (token0 // seq + b, 0))
    return q_spec, seq_spec


def _window_attention(qa, ka, va, bias, sink_rows, batch, seq, token0):
    tq = WIN_BLOCKS * BLOCK
    q_spec, _ = _stream_block_specs(tq, seq, token0, A_Q_COLS)
    _, k_spec = _stream_block_specs(tq, seq, token0, ka.shape[-1])
    _, v_spec = _stream_block_specs(tq, seq, token0, va.shape[-1])
    nq = seq // tq
    return pl.pallas_call(
        functools.partial(_win_kernel, seq=seq),
        grid=(batch, nq),
        in_specs=[q_spec, k_spec, v_spec, _const_spec(bias.shape), _const_spec(sink_rows.shape)],
        out_specs=pl.BlockSpec((tq, A_Q_COLS), lambda b, i: (b * nq + i, 0)),
        out_shape=jax.ShapeDtypeStruct((batch * seq, A_Q_COLS), BF16),
        scratch_shapes=[pltpu.VMEM((GROUP_A * BLOCK, 3 * BLOCK), F32),
                        pltpu.VMEM((GROUP_A * BLOCK, 3 * BLOCK), F32)],
        compiler_params=_params(2),
        name="window_attn",
    )(qa, ka, va, bias, sink_rows)


def _mla_kernel(q_ref, k_ref, v_ref, o_ref, s0_ref, s1_ref, *, seq):
    c2 = (QK_NOPE_DIM + QK_ROPE_DIM) ** -0.5 * math.log2(math.e)
    n_chunks = seq // MLA_KC
    sub = MLA_SUB
    lane = lax.broadcasted_iota(jnp.int32, (sub, SLOT), 1)
    s_bufs = (s0_ref, s1_ref)
    row0 = pl.multiple_of(jnp.minimum(pl.program_id(1), 0), 16)
    tile_rows = pl.ds(row0, sub)
    units = [(t, h) for t in range(q_ref.shape[0] // sub) for h in range(N_HEADS_B)]

    def scores(u):
        t, h = units[u]
        q_h = q_ref[t * sub:(t + 1) * sub, h * SLOT:(h + 1) * SLOT]
        m = jnp.full((sub, 1), -jnp.inf, F32)
        for c in range(n_chunks):
            rows = slice(c * MLA_KC, (c + 1) * MLA_KC)
            s = lax.dot_general(q_h, k_ref[rows, h * SLOT:(h + 1) * SLOT], (((1,), (1,)), ((), ())),
                                preferred_element_type=F32)
            s_bufs[u % 2][tile_rows, rows] = s
            m = jnp.maximum(m, jnp.max(s, axis=-1, keepdims=True))
        return m

    def weighted(u, m):
        _, h = units[u]
        lo = (h // 2) * 2 * SLOT
        acc = jnp.zeros((sub, 2 * SLOT), F32)
        for c in range(n_chunks):
            rows = slice(c * MLA_KC, (c + 1) * MLA_KC)
            p = jnp.exp2((s_bufs[u % 2][tile_rows, rows] - m) * c2)
            acc = acc + jnp.dot(p.astype(BF16), v_ref[rows, lo:lo + 2 * SLOT],
                                preferred_element_type=F32)
        return acc[:, (h % 2) * SLOT:(h % 2 + 1) * SLOT]

    m = scores(0)
    even = None
    for u, (t, h) in enumerate(units):
        m_next = scores(u + 1) if u + 1 < len(units) else None
        acc = weighted(u, m)
        out_h = acc / acc[:, 0:1]
        if h % 2 == 0:
            even = out_h
        else:
            pair = jnp.where(lane < V_DIM_B, pltpu.roll(even, SLOT - V_DIM_B, 1), out_h)
            o_ref[t * sub:(t + 1) * sub, (h // 2) * SLOT:(h // 2 + 1) * SLOT] = pair.astype(BF16)
        m = m_next


def _mla_attention(q, k, v, batch, seq, token0):
    tq = MLA_TQ
    hs = N_HEADS_B * SLOT
    oc = N_HEADS_B * V_DIM_B
    q_spec, seq_spec = _stream_block_specs(tq, seq, token0, hs)
    nq = seq // tq
    return pl.pallas_call(
        functools.partial(_mla_kernel, seq=seq),
        grid=(batch, nq),
        in_specs=[q_spec, seq_spec, seq_spec],
        out_specs=pl.BlockSpec((tq, oc), lambda b, i: (b * nq + i, 0)),
        out_shape=jax.ShapeDtypeStruct((batch * seq, oc), BF16),
        scratch_shapes=[pltpu.VMEM((MLA_SUB, seq), F32), pltpu.VMEM((MLA_SUB, seq), F32)],
        compiler_params=_params(2),
        name="mla_attn",
    )(q, k, v)


def _merge_kernel(x_ref, oa0_ref, oa1_ref, ob0_ref, ob1_ref, pre_ref, post_ref, wga_ref, wgb_ref,
                  wa_ref, wb_ref, wo_ref, o_ref, *, split):
    first = pl.program_id(0) < split
    for sub in range(x_ref.shape[0] // MERGE_SUB):
        rows = slice(sub * MERGE_SUB, (sub + 1) * MERGE_SUB)
        x = x_ref[rows, :]
        hb = _rms(x, pre_ref[...]).astype(BF16)
        oa = _stream_rows((oa0_ref, oa1_ref), rows, first)
        ob = _stream_rows((ob0_ref, ob1_ref), rows, first)
        ga = jax.nn.sigmoid(jnp.dot(hb, wga_ref[...], preferred_element_type=F32))
        merged = ga * jnp.dot(oa, wa_ref[...], preferred_element_type=F32)
        gb = jax.nn.sigmoid(jnp.dot(hb, wgb_ref[...], preferred_element_type=F32))
        merged = merged + gb * jnp.dot(ob, wb_ref[...], preferred_element_type=F32)
        y = jnp.dot(merged.astype(BF16), wo_ref[...], preferred_element_type=F32)
        o_ref[rows, :] = x + _rms(y, post_ref[...])


def _merge(x, oas, obs, layer, pre_g, post_g, wga, wgb, wa, wb, wo, split):
    t = x.shape[0]
    tm = MERGE_TM

    def row(cols):
        return pl.BlockSpec((tm, cols), lambda i: (i, 0))

    bcols = N_HEADS_B * V_DIM_B
    return pl.pallas_call(
        functools.partial(_merge_kernel, split=split),
        grid=(t // tm,),
        in_specs=[row(D_MODEL)] + _stream_specs(tm, A_Q_COLS, split, 2) + _stream_specs(tm, bcols, split, 2)
        + [_layer_spec((1, D_MODEL), layer), _layer_spec((1, D_MODEL), layer),
           _const_spec((D_MODEL, D_MODEL)), _const_spec((D_MODEL, D_MODEL)),
           _layer_spec((A_Q_COLS, D_MODEL), layer), _layer_spec((bcols, D_MODEL), layer),
           _layer_spec((D_MODEL, D_MODEL), layer)],
        out_specs=row(D_MODEL),
        out_shape=jax.ShapeDtypeStruct((t, D_MODEL), F32),
        compiler_params=_params(1),
        name="merge",
    )(x, *oas, *obs, pre_g, post_g, wga, wgb, wa, wb, wo)


def _layer_weights(wi, uq):
    o = 0
    pieces = {}
    for name, sz in (("qa", A_Q_COLS), ("ka", A_KV_COLS), ("va", A_KV_COLS), ("cq", Q_LORA_RANK),
                     ("ckv", KV_LORA_RANK), ("kr", QK_ROPE_DIM), ("ga", D_MODEL), ("gb", D_MODEL)):
        pieces[name] = wi[:, o:o + sz]
        o += sz
    z64 = jnp.zeros((D_MODEL, 64), BF16)
    z32 = jnp.zeros((D_MODEL, 32), BF16)
    w1 = jnp.concatenate([
        pieces["qa"] * (HEAD_DIM_A ** -0.5),
        pieces["ka"], pieces["va"], pieces["cq"], pieces["ckv"],
        z64, pieces["kr"], z32,
    ], axis=1)

    uq = uq.reshape(Q_LORA_RANK, N_HEADS_B, QK_NOPE_DIM + QK_ROPE_DIM)
    q_rope = uq[..., QK_NOPE_DIM:]
    r1, r2 = q_rope[..., :QK_ROPE_DIM // 2], q_rope[..., QK_ROPE_DIM // 2:]
    zq = jnp.zeros((Q_LORA_RANK, N_HEADS_B, SLOT - QK_NOPE_DIM - QK_ROPE_DIM), BF16)
    zn = jnp.zeros((Q_LORA_RANK, N_HEADS_B, QK_NOPE_DIM), BF16)
    q_main = jnp.concatenate([uq, zq], axis=-1).reshape(Q_LORA_RANK, -1)
    q_swap = jnp.concatenate([zn, r2, r1, zq], axis=-1).reshape(Q_LORA_RANK, -1)
    wq = jnp.concatenate([q_main, q_swap], axis=1)
    return w1, wq, pieces["ga"], pieces["gb"]


def _sink_rows(sink_l):
    col = jnp.repeat(sink_l, BLOCK).reshape(N_KV_HEADS_A, GROUP_A * BLOCK, 1)
    lane = jnp.arange(SLOT)[None, None, :]
    return jnp.where(lane == 0, col, -jnp.inf).astype(F32)


def _rope_slot_tables(seq):
    half = QK_ROPE_DIM // 2
    inv_freq = ROPE_THETA ** (-np.arange(0, QK_ROPE_DIM, 2, dtype=np.float64) / QK_ROPE_DIM)
    ang = np.arange(seq, dtype=np.float64)[:, None] * inv_freq[None, :]
    cos, sin = np.cos(ang), np.sin(ang)
    z_half = np.zeros((seq, half))
    z_nope = np.zeros((seq, QK_NOPE_DIM))
    pad = np.zeros((seq, SLOT - QK_NOPE_DIM - QK_ROPE_DIM))
    cos_tab = np.concatenate([np.ones((seq, QK_NOPE_DIM)), cos, cos, pad], axis=1)
    sin_lo = np.concatenate([z_nope, -sin, z_half, pad], axis=1)
    sin_hi = np.concatenate([z_nope, z_half, sin, pad], axis=1)
    return tuple(jnp.asarray(t, dtype=F32) for t in (cos_tab, sin_lo, sin_hi))


def _trunk(x_first, x_second, bias, shared, layers):
    shapes = (x_first.shape, x_second.shape)
    xs = tuple(x.reshape(-1, D_MODEL) for x in (x_first, x_second))
    t0 = xs[0].shape[0]
    assert FFN_TM == PROJ_TM == MERGE_TM and t0 % FFN_TM == 0 and xs[1].shape[0] % FFN_TM == 0
    split = t0 // FFN_TM
    seqs = (shapes[0][1], shapes[1][1])
    streams = ((shapes[0][0], seqs[0], 0), (shapes[1][0], seqs[1], t0))
    rope_tabs = _rope_slot_tables(max(seqs))
    for l, lw in enumerate(layers):
        x = _ffn(xs, l, *shared["ffn1"], split)
        qa, ka, va, q, k, v = _proj(x, l, shared["mix_pre_g"], lw["w1"], shared["q_norm_g"],
                                    shared["kv_norm_g"], lw["wq"], shared["w_ukv"], rope_tabs, seqs, split)
        oas = [_window_attention(qa, ka, va, bias, lw["sink_rows"], *s) for s in streams]
        obs = [_mla_attention(q, k, v, *s) for s in streams]
        x = _merge(x, oas, obs, l, shared["mix_pre_g"], shared["mix_post_g"], lw["wga"], lw["wgb"],
                   shared["w_a_out"], shared["w_b_out"], shared["w_o"], split)
        last = l == len(layers) - 1
        xs = _ffn((x,), l, *shared["ffn2"], split, n_out=2 if last else 1)
        if not last:
            xs = (xs,)
    return tuple(y.reshape(shape) for y, shape in zip(xs, shapes))


def kernel(x_prompt, x_sample, rel_bias, ffn1_pre_g, ffn1_post_g, ffn1_w_gate, ffn1_w_up, ffn1_w_down, mix_pre_g, mix_post_g, w_in, sink, q_norm_g, kv_norm_g, w_uq, w_ukv, w_a_out, w_b_out, w_o, ffn2_pre_g, ffn2_post_g, ffn2_w_gate, ffn2_w_up, ffn2_w_down):
    def gains(g):
        return g.reshape(DEPTH, 1, -1)

    def bf16(w):
        return w.astype(BF16)

    shared = dict(
        ffn1=(gains(ffn1_pre_g), gains(ffn1_post_g), ffn1_w_gate, ffn1_w_up, ffn1_w_down),
        ffn2=(gains(ffn2_pre_g), gains(ffn2_post_g), ffn2_w_gate, ffn2_w_up, ffn2_w_down),
        mix_pre_g=gains(mix_pre_g), mix_post_g=gains(mix_post_g),
        q_norm_g=gains(q_norm_g), kv_norm_g=gains(kv_norm_g),
        w_a_out=bf16(w_a_out), w_b_out=bf16(w_b_out), w_o=bf16(w_o), w_ukv=bf16(w_ukv),
    )
    w_in_b, w_uq_b = bf16(w_in), bf16(w_uq)
    layers = []
    for l in range(DEPTH):
        w1, wq, wga, wgb = _layer_weights(w_in_b[l], w_uq_b[l])
        layers.append(dict(w1=w1, wq=wq, wga=wga, wgb=wgb, sink_rows=_sink_rows(sink[l])))
    bias = _bias_table(rel_bias, _band_buckets())
    bias = bias.reshape(3, N_KV_HEADS_A, GROUP_A * BLOCK, 3 * BLOCK)
    return _trunk(x_prompt, x_sample, bias, shared, layers)
```

```python
import functools
import math

import jax
import jax.numpy as jnp
import numpy as np
from jax import lax
from jax.experimental import pallas as pl
from jax.experimental.pallas import tpu as pltpu

F32 = jnp.float32
BF16 = jnp.bfloat16

D_MODEL = 1024
DEPTH = 2
N_HEADS_A = 8
N_KV_HEADS_A = 2
GROUP_A = N_HEADS_A // N_KV_HEADS_A
HEAD_DIM_A = 64
WINDOW = 128
BLOCK = 128
N_HEADS_B = 8
QK_NOPE_DIM = 64
QK_ROPE_DIM = 32
V_DIM_B = 64
Q_LORA_RANK = 384
KV_LORA_RANK = 256
ROPE_THETA = 10000.0
N_BUCKETS = 32
MAX_DISTANCE = 128
D_FF = 2816
FFN_RES_WEIGHT = 0.5
EPS = 1e-6

A_Q_COLS = N_HEADS_A * HEAD_DIM_A
A_KV_COLS = N_KV_HEADS_A * HEAD_DIM_A

LANES = 128
SLOT = LANES
VMEM_LIMIT_BYTES = 56 * 1024 * 1024

W1_QA = (0, 512)
W1_KA = (512, 640)
W1_VA = (640, 768)
W1_CQ = (768, 1152)
W1_CKV = (1152, 1408)
W1_KR = (1408, 1536)
W1_COLS = 1536

FFN_CHUNKS = ((0, 1024), (1024, 1024), (2048, 768))
FFN_WEIGHT_PIECES = 16
FFN_STAGE_SLOTS = 4
ROW_SUB = 256
MERGE_SUB = 256
PROJ_SUB = 1024
FFN_TM = 1024
PROJ_TM = 1024
MERGE_TM = 1024
MLA_TQ = 512
MLA_SUB = 256
MLA_KC = 1024
WIN_BLOCKS = 16


def _rms(x, g):
    return x * lax.rsqrt(jnp.mean(x * x, axis=-1, keepdims=True) + EPS) * g


def _const_spec(shape):
    nd = len(shape)
    return pl.BlockSpec(shape, lambda *_: (0,) * nd, pipeline_mode=pl.Buffered(1))


def _layer_spec(shape, layer):
    nd = len(shape)
    return pl.BlockSpec((None,) + tuple(shape), lambda *_: (layer,) + (0,) * nd,
                        pipeline_mode=pl.Buffered(1))


def _params(n_axes, flags=None):
    return pltpu.CompilerParams(
        dimension_semantics=("parallel",) * n_axes,
        vmem_limit_bytes=VMEM_LIMIT_BYTES,
        flags=flags,
    )


def _stream_specs(tm, cols, split, n):
    if n == 1:
        return [pl.BlockSpec((tm, cols), lambda i: (i, 0))]
    return [pl.BlockSpec((tm, cols), lambda i: (jnp.minimum(i, split - 1), 0)),
            pl.BlockSpec((tm, cols), lambda i: (jnp.maximum(i - split, 0), 0))]


def _stream_rows(refs, rows, first):
    if len(refs) == 1:
        return refs[0][rows, :]
    return jnp.where(first, refs[0][rows, :], refs[1][rows, :])


def _load_weight_bf16(w_hbm, w_ref, stage_ref, sem_ref):
    n_slots, rows = stage_ref.shape[0], stage_ref.shape[1]
    n_pieces = w_ref.shape[0] // rows

    def piece_copy(k):
        slot = k % n_slots
        return pltpu.make_async_copy(w_hbm.at[pl.ds(k * rows, rows), :], stage_ref.at[slot],
                                     sem_ref.at[slot])

    for k in range(min(n_slots - 1, n_pieces)):
        piece_copy(k).start()
    for k in range(n_pieces):
        if k + n_slots - 1 < n_pieces:
            piece_copy(k + n_slots - 1).start()
        piece_copy(k).wait()
        w_ref[k * rows:(k + 1) * rows, :] = stage_ref[k % n_slots].astype(BF16)


def _ffn_kernel(*refs, n_in, n_out, split, layer):
    x_refs, refs = refs[:n_in], refs[n_in:]
    pre_ref, post_ref, wg_hbm, wu_hbm, wd_hbm = refs[:5]
    o_refs, refs = refs[5:5 + n_out], refs[5 + n_out:]
    acc_ref, wg_ref, wu_ref, wd_ref, stage_in_ref, stage_out_ref, sem_ref = refs
    first = pl.program_id(0) < split

    @pl.when(pl.program_id(0) == 0)
    def _():
        _load_weight_bf16(wg_hbm.at[layer], wg_ref, stage_in_ref, sem_ref)
        _load_weight_bf16(wu_hbm.at[layer], wu_ref, stage_in_ref, sem_ref)
        _load_weight_bf16(wd_hbm.at[layer], wd_ref, stage_out_ref, sem_ref)

    for sub in range(x_refs[0].shape[0] // ROW_SUB):
        rows = slice(sub * ROW_SUB, (sub + 1) * ROW_SUB)
        x = _stream_rows(x_refs, rows, first)
        hb = _rms(x, pre_ref[...]).astype(BF16)
        for idx, (c0, cw) in enumerate(FFN_CHUNKS):
            g = jnp.dot(hb, wg_ref[:, c0:c0 + cw], preferred_element_type=F32)
            u = jnp.dot(hb, wu_ref[:, c0:c0 + cw], preferred_element_type=F32)
            a = (jax.nn.silu(g) * u).astype(BF16)
            part = jnp.dot(a, wd_ref[c0:c0 + cw, :], preferred_element_type=F32)
            if idx == 0:
                acc_ref[rows, :] = part
            else:
                acc_ref[rows, :] += part
        y = x + FFN_RES_WEIGHT * _rms(acc_ref[rows, :], post_ref[...])
        o_refs[-1][rows, :] = y
    if n_out == 2:
        @pl.when(first)
        def _():
            o_refs[0][...] = o_refs[1][...]


def _ffn(xs, layer, pre_g, post_g, wg, wu, wd, split, n_out=1):
    tm = FFN_TM
    t = sum(x.shape[0] for x in xs)
    n_in = len(xs)
    hbm = pl.BlockSpec(memory_space=pl.ANY)
    if n_out == 1:
        out_shape = [jax.ShapeDtypeStruct((t, D_MODEL), F32)]
    else:
        out_shape = [jax.ShapeDtypeStruct((split * tm, D_MODEL), F32),
                     jax.ShapeDtypeStruct((t - split * tm, D_MODEL), F32)]
    outs = pl.pallas_call(
        functools.partial(_ffn_kernel, n_in=n_in, n_out=n_out, split=split, layer=layer),
        grid=(t // tm,),
        in_specs=_stream_specs(tm, D_MODEL, split, n_in)
        + [_layer_spec((1, D_MODEL), layer), _layer_spec((1, D_MODEL), layer), hbm, hbm, hbm],
        out_specs=_stream_specs(tm, D_MODEL, split, n_out),
        out_shape=out_shape,
        scratch_shapes=[pltpu.VMEM((tm, D_MODEL), F32),
                        pltpu.VMEM((D_MODEL, D_FF), BF16), pltpu.VMEM((D_MODEL, D_FF), BF16),
                        pltpu.VMEM((D_FF, D_MODEL), BF16),
                        pltpu.VMEM((2 * FFN_STAGE_SLOTS, D_MODEL // (2 * FFN_WEIGHT_PIECES), D_FF), F32),
                        pltpu.VMEM((FFN_STAGE_SLOTS, D_FF // FFN_WEIGHT_PIECES, D_MODEL), F32),
                        pltpu.SemaphoreType.DMA((2 * FFN_STAGE_SLOTS,))],
        compiler_params=pltpu.CompilerParams(dimension_semantics=("arbitrary",),
                                             vmem_limit_bytes=VMEM_LIMIT_BYTES),
        name="ffn",
    )(*xs, pre_g, post_g, wg, wu, wd)
    return outs[0] if n_out == 1 else outs


def _rope(x, cos, sin_lo, sin_hi):
    half = QK_ROPE_DIM // 2
    return x * cos + pltpu.roll(x, SLOT - half, 1) * sin_lo + pltpu.roll(x, half, 1) * sin_hi


def _proj_kernel(x_ref, g_ref, w1_ref, qg_ref, kvg_ref, wq_ref, wkv_ref, cos_ref, sin_lo_ref, sin_hi_ref,
                 qa_ref, ka_ref, va_ref, q_ref, k_ref, v_ref):
    hs = N_HEADS_B * SLOT
    half = SLOT // 2
    lane = lax.broadcasted_iota(jnp.int32, (PROJ_SUB, SLOT), 1)
    low = lane < half
    one_at_zero = jnp.where(lane == 0, 1.0, 0.0)

    def dup_halves(x):
        swapped = pltpu.roll(x, half, 1)
        return jnp.where(low, x, swapped), jnp.where(low, swapped, x)

    for sub in range(x_ref.shape[0] // PROJ_SUB):
        rows = slice(sub * PROJ_SUB, (sub + 1) * PROJ_SUB)
        hb = _rms(x_ref[rows, :], g_ref[...]).astype(BF16)
        proj = jnp.dot(hb, w1_ref[...], preferred_element_type=F32)
        qa_ref[rows, :] = proj[:, W1_QA[0]:W1_QA[1]].astype(BF16)
        for src, dst in ((W1_KA, ka_ref), (W1_VA, va_ref)):
            first, second = dup_halves(proj[:, src[0]:src[1]])
            dst[rows, :SLOT] = first.astype(BF16)
            dst[rows, SLOT:] = second.astype(BF16)
        cqn = _rms(proj[:, W1_CQ[0]:W1_CQ[1]], qg_ref[...]).astype(BF16)
        ckvn = _rms(proj[:, W1_CKV[0]:W1_CKV[1]], kvg_ref[...]).astype(BF16)
        cos = cos_ref[rows, :]
        sin_lo = sin_lo_ref[rows, :]
        sin_hi = sin_hi_ref[rows, :]
        k_rope = _rope(proj[:, W1_KR[0]:W1_KR[1]], cos, sin_lo, sin_hi)
        qf = jnp.dot(cqn, wq_ref[...], preferred_element_type=F32)
        sin_signed = sin_lo + sin_hi
        kvf = jnp.dot(ckvn, wkv_ref[...], preferred_element_type=F32)
        for h in range(N_HEADS_B):
            lo, hi = h * SLOT, (h + 1) * SLOT
            q_h = qf[:, lo:hi] * cos + qf[:, hs + lo:hs + hi] * sin_signed
            q_ref[rows, lo:hi] = q_h.astype(BF16)
            kv_h = kvf[:, lo:hi]
            k_ref[rows, lo:hi] = jnp.where(low, kv_h, k_rope).astype(BF16)
            v_ref[rows, lo:hi] = jnp.where(low, one_at_zero, kv_h).astype(BF16)


def _proj(x, layer, g, w1, qg, kvg, wq, wkv, rope_tabs, seqs, split):
    t = x.shape[0]
    tm = PROJ_TM
    n0, n1 = seqs[0] // tm, seqs[1] // tm

    def row(cols):
        return pl.BlockSpec((tm, cols), lambda i: (i, 0))

    tab = pl.BlockSpec((tm, SLOT), lambda i: (jnp.where(i < split, i % n0, (i - split) % n1), 0))
    hs = N_HEADS_B * SLOT
    out_cols = (A_Q_COLS, 2 * SLOT, 2 * SLOT, hs, hs, hs)
    return pl.pallas_call(
        _proj_kernel,
        grid=(t // tm,),
        in_specs=[row(D_MODEL), _layer_spec((1, D_MODEL), layer), _const_spec((D_MODEL, W1_COLS)),
                  _layer_spec((1, Q_LORA_RANK), layer), _layer_spec((1, KV_LORA_RANK), layer),
                  _const_spec((Q_LORA_RANK, 2 * hs)), _layer_spec((KV_LORA_RANK, hs), layer),
                  tab, tab, tab],
        out_specs=[row(c) for c in out_cols],
        out_shape=[jax.ShapeDtypeStruct((t, c), BF16) for c in out_cols],
        compiler_params=_params(1),
        name="proj",
    )(x, g, w1, qg, kvg, wq, wkv, *rope_tabs)


def _bias_kernel(rb_ref, bucket_ref, o_ref):
    bucket = bucket_ref[0]
    for h in range(N_HEADS_A):
        acc = jnp.full(bucket.shape, -jnp.inf, F32)
        for b in range(N_BUCKETS):
            acc = jnp.where(bucket == b, rb_ref[b, h], acc)
        o_ref[0, h] = acc


def _bias_table(rel_bias, bucket):
    kw = 3 * BLOCK
    return pl.pallas_call(
        _bias_kernel,
        grid=(3,),
        in_specs=[pl.BlockSpec(memory_space=pltpu.SMEM),
                  pl.BlockSpec((1, BLOCK, kw), lambda e: (e, 0, 0))],
        out_specs=pl.BlockSpec((1, N_HEADS_A, BLOCK, kw), lambda e: (e, 0, 0, 0)),
        out_shape=jax.ShapeDtypeStruct((3, N_HEADS_A, BLOCK, kw), F32),
        compiler_params=_params(1),
        name="t5_bias",
    )(rel_bias, bucket)


def _t5_bucket(rel):
    half = N_BUCKETS // 2
    max_exact = half // 2
    ret = jnp.where(rel > 0, half, 0)
    n = jnp.abs(rel)
    nf = jnp.maximum(n, 1).astype(F32)
    large = max_exact + (jnp.log(nf / max_exact) / math.log(MAX_DISTANCE / max_exact)
                         * (half - max_exact)).astype(jnp.int32)
    large = jnp.minimum(large, half - 1)
    return ret + jnp.where(n < max_exact, n, large)


def _band_buckets():
    qi = jnp.arange(BLOCK, dtype=jnp.int32)[:, None]
    si = jnp.arange(3 * BLOCK, dtype=jnp.int32)[None, :]
    out = []
    for shift in (0, -BLOCK, -2 * BLOCK):
        rel = si + shift - qi
        out.append(jnp.where(jnp.abs(rel) <= WINDOW, _t5_bucket(rel), -1))
    return jnp.stack(out).astype(jnp.int32)


def _win_kernel(q_ref, k_ref, v_ref, bias_ref, sink_ref, o_ref, s0_ref, s1_ref, *, seq):
    step = pl.program_id(1)
    nb = seq // BLOCK
    kw = 3 * BLOCK
    gw = GROUP_A * HEAD_DIM_A
    lane = lax.broadcasted_iota(jnp.int32, (BLOCK, gw), 1)
    keep = [(lane >= r * HEAD_DIM_A) & (lane < (r + 1) * HEAD_DIM_A) for r in range(GROUP_A)]
    s_bufs = (s0_ref, s1_ref)
    row0 = pl.multiple_of(jnp.minimum(step, 0), 8)
    tile_rows = pl.ds(row0, GROUP_A * BLOCK)
    units = [(blk, g) for blk in range(WIN_BLOCKS) for g in range(N_KV_HEADS_A)]
    windows = []
    for blk in range(WIN_BLOCKS):
        n = step * WIN_BLOCKS + blk
        start = pl.multiple_of(jnp.clip((n - 1) * BLOCK, 0, seq - kw), BLOCK)
        windows.append((start, jnp.where(n == 0, 0, jnp.where(n == nb - 1, 2, 1))))

    def scores(u):
        blk, g = units[u]
        start, edge = windows[blk]
        q_grp = q_ref[blk * BLOCK:(blk + 1) * BLOCK, g * gw:(g + 1) * gw]
        q_stack = jnp.concatenate(
            [jnp.where(keep[r], q_grp, jnp.zeros_like(q_grp)) for r in range(GROUP_A)], axis=0)
        k_dup = k_ref[pl.ds(start, kw), g * SLOT:(g + 1) * SLOT]
        k_rep = jnp.concatenate([k_dup, k_dup], axis=1)
        s = lax.dot_general(q_stack, k_rep, (((1,), (1,)), ((), ())),
                            preferred_element_type=F32) + bias_ref[edge, g]
        s_bufs[u % 2][tile_rows, :] = s
        return jnp.max(jnp.concatenate([s, sink_ref[g]], axis=1), axis=-1, keepdims=True)

    def weighted(u, m):
        blk, g = units[u]
        start, _ = windows[blk]
        v_dup = v_ref[pl.ds(start, kw), g * SLOT:(g + 1) * SLOT]
        v_rep = jnp.concatenate([v_dup, v_dup], axis=1)
        p = jnp.exp(s_bufs[u % 2][tile_rows, :] - m)
        denom = (jnp.sum(p, axis=-1, keepdims=True)
                 + jnp.sum(jnp.exp(sink_ref[g] - m), axis=-1, keepdims=True))
        o_all = jnp.dot(p.astype(BF16), v_rep, preferred_element_type=F32) / denom
        out = jnp.zeros((BLOCK, gw), F32)
        for r in range(GROUP_A):
            out = jnp.where(keep[r], o_all[r * BLOCK:(r + 1) * BLOCK], out)
        o_ref[blk * BLOCK:(blk + 1) * BLOCK, g * gw:(g + 1) * gw] = out.astype(BF16)

    m = scores(0)
    for u in range(len(units)):
        m_next = scores(u + 1) if u + 1 < len(units) else None
        weighted(u, m)
        m = m_next


def _stream_block_specs(tq, seq, token0, cols):
    assert token0 % seq == 0 and seq % tq == 0
    nq = seq // tq
    q_spec = pl.BlockSpec((tq, cols), lambda b, i: (token0 // tq + b * nq + i, 0))
    seq_spec = pl.BlockSpec((seq, cols), lambda b, i: (token0 // seq + b, 0))
    return q_spec, seq_spec


def _window_attention(qa, ka, va, bias, sink_rows, batch, seq, token0):
    tq = WIN_BLOCKS * BLOCK
    q_spec, _ = _stream_block_specs(tq, seq, token0, A_Q_COLS)
    _, k_spec = _stream_block_specs(tq, seq, token0, ka.shape[-1])
    _, v_spec = _stream_block_specs(tq, seq, token0, va.shape[-1])
    nq = seq // tq
    return pl.pallas_call(
        functools.partial(_win_kernel, seq=seq),
        grid=(batch, nq),
        in_specs=[q_spec, k_spec, v_spec, _const_spec(bias.shape), _const_spec(sink_rows.shape)],
        out_specs=pl.BlockSpec((tq, A_Q_COLS), lambda b, i: (b * nq + i, 0)),
        out_shape=jax.ShapeDtypeStruct((batch * seq, A_Q_COLS), BF16),
        scratch_shapes=[pltpu.VMEM((GROUP_A * BLOCK, 3 * BLOCK), F32),
                        pltpu.VMEM((GROUP_A * BLOCK, 3 * BLOCK), F32)],
        compiler_params=_params(2),
        name="window_attn",
    )(qa, ka, va, bias, sink_rows)


def _mla_kernel(q_ref, k_ref, v_ref, o_ref, s0_ref, s1_ref, *, seq):
    c2 = (QK_NOPE_DIM + QK_ROPE_DIM) ** -0.5 * math.log2(math.e)
    n_chunks = seq // MLA_KC
    sub = MLA_SUB
    lane = lax.broadcasted_iota(jnp.int32, (sub, SLOT), 1)
    s_bufs = (s0_ref, s1_ref)
    row0 = pl.multiple_of(jnp.minimum(pl.program_id(1), 0), 16)
    tile_rows = pl.ds(row0, sub)
    units = [(t, h) for t in range(q_ref.shape[0] // sub) for h in range(N_HEADS_B)]

    def scores(u):
        t, h = units[u]
        q_h = q_ref[t * sub:(t + 1) * sub, h * SLOT:(h + 1) * SLOT]
        m = jnp.full((sub, 1), -jnp.inf, F32)
        for c in range(n_chunks):
            rows = slice(c * MLA_KC, (c + 1) * MLA_KC)
            s = lax.dot_general(q_h, k_ref[rows, h * SLOT:(h + 1) * SLOT], (((1,), (1,)), ((), ())),
                                preferred_element_type=F32)
            s_bufs[u % 2][tile_rows, rows] = s
            m = jnp.maximum(m, jnp.max(s, axis=-1, keepdims=True))
        return m

    def weighted(u, m):
        _, h = units[u]
        lo = (h // 2) * 2 * SLOT
        acc = jnp.zeros((sub, 2 * SLOT), F32)
        for c in range(n_chunks):
            rows = slice(c * MLA_KC, (c + 1) * MLA_KC)
            p = jnp.exp2((s_bufs[u % 2][tile_rows, rows] - m) * c2)
            acc = acc + jnp.dot(p.astype(BF16), v_ref[rows, lo:lo + 2 * SLOT],
                                preferred_element_type=F32)
        return acc[:, (h % 2) * SLOT:(h % 2 + 1) * SLOT]

    m = scores(0)
    even = None
    for u, (t, h) in enumerate(units):
        m_next = scores(u + 1) if u + 1 < len(units) else None
        acc = weighted(u, m)
        out_h = acc / acc[:, 0:1]
        if h % 2 == 0:
            even = out_h
        else:
            pair = jnp.where(lane < V_DIM_B, pltpu.roll(even, SLOT - V_DIM_B, 1), out_h)
            o_ref[t * sub:(t + 1) * sub, (h // 2) * SLOT:(h // 2 + 1) * SLOT] = pair.astype(BF16)
        m = m_next


def _mla_attention(q, k, v, batch, seq, token0):
    tq = MLA_TQ
    hs = N_HEADS_B * SLOT
    oc = N_HEADS_B * V_DIM_B
    q_spec, seq_spec = _stream_block_specs(tq, seq, token0, hs)
    nq = seq // tq
    return pl.pallas_call(
        functools.partial(_mla_kernel, seq=seq),
        grid=(batch, nq),
        in_specs=[q_spec, seq_spec, seq_spec],
        out_specs=pl.BlockSpec((tq, oc), lambda b, i: (b * nq + i, 0)),
        out_shape=jax.ShapeDtypeStruct((batch * seq, oc), BF16),
        scratch_shapes=[pltpu.VMEM((MLA_SUB, seq), F32), pltpu.VMEM((MLA_SUB, seq), F32)],
        compiler_params=_params(2),
        name="mla_attn",
    )(q, k, v)


def _merge_kernel(x_ref, oa0_ref, oa1_ref, ob0_ref, ob1_ref, pre_ref, post_ref, wga_ref, wgb_ref,
                  wa_ref, wb_ref, wo_ref, o_ref, *, split):
    first = pl.program_id(0) < split
    for sub in range(x_ref.shape[0] // MERGE_SUB):
        rows = slice(sub * MERGE_SUB, (sub + 1) * MERGE_SUB)
        x = x_ref[rows, :]
        hb = _rms(x, pre_ref[...]).astype(BF16)
        oa = _stream_rows((oa0_ref, oa1_ref), rows, first)
        ob = _stream_rows((ob0_ref, ob1_ref), rows, first)
        ga = jax.nn.sigmoid(jnp.dot(hb, wga_ref[...], preferred_element_type=F32))
        merged = ga * jnp.dot(oa, wa_ref[...], preferred_element_type=F32)
        gb = jax.nn.sigmoid(jnp.dot(hb, wgb_ref[...], preferred_element_type=F32))
        merged = merged + gb * jnp.dot(ob, wb_ref[...], preferred_element_type=F32)
        y = jnp.dot(merged.astype(BF16), wo_ref[...], preferred_element_type=F32)
        o_ref[rows, :] = x + _rms(y, post_ref[...])


def _merge(x, oas, obs, layer, pre_g, post_g, wga, wgb, wa, wb, wo, split):
    t = x.shape[0]
    tm = MERGE_TM

    def row(cols):
        return pl.BlockSpec((tm, cols), lambda i: (i, 0))

    bcols = N_HEADS_B * V_DIM_B
    return pl.pallas_call(
        functools.partial(_merge_kernel, split=split),
        grid=(t // tm,),
        in_specs=[row(D_MODEL)] + _stream_specs(tm, A_Q_COLS, split, 2) + _stream_specs(tm, bcols, split, 2)
        + [_layer_spec((1, D_MODEL), layer), _layer_spec((1, D_MODEL), layer),
           _const_spec((D_MODEL, D_MODEL)), _const_spec((D_MODEL, D_MODEL)),
           _layer_spec((A_Q_COLS, D_MODEL), layer), _layer_spec((bcols, D_MODEL), layer),
           _layer_spec((D_MODEL, D_MODEL), layer)],
        out_specs=row(D_MODEL),
        out_shape=jax.ShapeDtypeStruct((t, D_MODEL), F32),
        compiler_params=_params(1),
        name="merge",
    )(x, *oas, *obs, pre_g, post_g, wga, wgb, wa, wb, wo)


def _layer_weights(wi, uq):
    o = 0
    pieces = {}
    for name, sz in (("qa", A_Q_COLS), ("ka", A_KV_COLS), ("va", A_KV_COLS), ("cq", Q_LORA_RANK),
                     ("ckv", KV_LORA_RANK), ("kr", QK_ROPE_DIM), ("ga", D_MODEL), ("gb", D_MODEL)):
        pieces[name] = wi[:, o:o + sz]
        o += sz
    z64 = jnp.zeros((D_MODEL, 64), BF16)
    z32 = jnp.zeros((D_MODEL, 32), BF16)
    w1 = jnp.concatenate([
        pieces["qa"] * (HEAD_DIM_A ** -0.5),
        pieces["ka"], pieces["va"], pieces["cq"], pieces["ckv"],
        z64, pieces["kr"], z32,
    ], axis=1)

    uq = uq.reshape(Q_LORA_RANK, N_HEADS_B, QK_NOPE_DIM + QK_ROPE_DIM)
    q_rope = uq[..., QK_NOPE_DIM:]
    r1, r2 = q_rope[..., :QK_ROPE_DIM // 2], q_rope[..., QK_ROPE_DIM // 2:]
    zq = jnp.zeros((Q_LORA_RANK, N_HEADS_B, SLOT - QK_NOPE_DIM - QK_ROPE_DIM), BF16)
    zn = jnp.zeros((Q_LORA_RANK, N_HEADS_B, QK_NOPE_DIM), BF16)
    q_main = jnp.concatenate([uq, zq], axis=-1).reshape(Q_LORA_RANK, -1)
    q_swap = jnp.concatenate([zn, r2, r1, zq], axis=-1).reshape(Q_LORA_RANK, -1)
    wq = jnp.concatenate([q_main, q_swap], axis=1)
    return w1, wq, pieces["ga"], pieces["gb"]


def _sink_rows(sink_l):
    col = jnp.repeat(sink_l, BLOCK).reshape(N_KV_HEADS_A, GROUP_A * BLOCK, 1)
    lane = jnp.arange(SLOT)[None, None, :]
    return jnp.where(lane == 0, col, -jnp.inf).astype(F32)


def _rope_slot_tables(seq):
    half = QK_ROPE_DIM // 2
    inv_freq = ROPE_THETA ** (-np.arange(0, QK_ROPE_DIM, 2, dtype=np.float64) / QK_ROPE_DIM)
    ang = np.arange(seq, dtype=np.float64)[:, None] * inv_freq[None, :]
    cos, sin = np.cos(ang), np.sin(ang)
    z_half = np.zeros((seq, half))
    z_nope = np.zeros((seq, QK_NOPE_DIM))
    pad = np.zeros((seq, SLOT - QK_NOPE_DIM - QK_ROPE_DIM))
    cos_tab = np.concatenate([np.ones((seq, QK_NOPE_DIM)), cos, cos, pad], axis=1)
    sin_lo = np.concatenate([z_nope, -sin, z_half, pad], axis=1)
    sin_hi = np.concatenate([z_nope, z_half, sin, pad], axis=1)
    return tuple(jnp.asarray(t, dtype=F32) for t in (cos_tab, sin_lo, sin_hi))


def _trunk(x_first, x_second, bias, shared, layers):
    shapes = (x_first.shape, x_second.shape)
    xs = tuple(x.reshape(-1, D_MODEL) for x in (x_first, x_second))
    t0 = xs[0].shape[0]
    assert FFN_TM == PROJ_TM == MERGE_TM and t0 % FFN_TM == 0 and xs[1].shape[0] % FFN_TM == 0
    split = t0 // FFN_TM
    seqs = (shapes[0][1], shapes[1][1])
    streams = ((shapes[0][0], seqs[0], 0), (shapes[1][0], seqs[1], t0))
    rope_tabs = _rope_slot_tables(max(seqs))
    for l, lw in enumerate(layers):
        x = _ffn(xs, l, *shared["ffn1"], split)
        qa, ka, va, q, k, v = _proj(x, l, shared["mix_pre_g"], lw["w1"], shared["q_norm_g"],
                                    shared["kv_norm_g"], lw["wq"], shared["w_ukv"], rope_tabs, seqs, split)
        oas = [_window_attention(qa, ka, va, bias, lw["sink_rows"], *s) for s in streams]
        obs = [_mla_attention(q, k, v, *s) for s in streams]
        x = _merge(x, oas, obs, l, shared["mix_pre_g"], shared["mix_post_g"], lw["wga"], lw["wgb"],
                   shared["w_a_out"], shared["w_b_out"], shared["w_o"], split)
        last = l == len(layers) - 1
        xs = _ffn((x,), l, *shared["ffn2"], split, n_out=2 if last else 1)
        if not last:
            xs = (xs,)
    return tuple(y.reshape(shape) for y, shape in zip(xs, shapes))


def kernel(x_prompt, x_sample, rel_bias, ffn1_pre_g, ffn1_post_g, ffn1_w_gate, ffn1_w_up, ffn1_w_down, mix_pre_g, mix_post_g, w_in, sink, q_norm_g, kv_norm_g, w_uq, w_ukv, w_a_out, w_b_out, w_o, ffn2_pre_g, ffn2_post_g, ffn2_w_gate, ffn2_w_up, ffn2_w_down):
    def gains(g):
        return g.reshape(DEPTH, 1, -1)

    def bf16(w):
        return w.astype(BF16)

    shared = dict(
        ffn1=(gains(ffn1_pre_g), gains(ffn1_post_g), ffn1_w_gate, ffn1_w_up, ffn1_w_down),
        ffn2=(gains(ffn2_pre_g), gains(ffn2_post_g), ffn2_w_gate, ffn2_w_up, ffn2_w_down),
        mix_pre_g=gains(mix_pre_g), mix_post_g=gains(mix_post_g),
        q_norm_g=gains(q_norm_g), kv_norm_g=gains(kv_norm_g),
        w_a_out=bf16(w_a_out), w_b_out=bf16(w_b_out), w_o=bf16(w_o), w_ukv=bf16(w_ukv),
    )
    w_in_b, w_uq_b = bf16(w_in), bf16(w_uq)
    layers = []
    for l in range(DEPTH):
        w1, wq, wga, wgb = _layer_weights(w_in_b[l], w_uq_b[l])
        layers.append(dict(w1=w1, wq=wq, wga=wga, wgb=wgb, sink_rows=_sink_rows(sink[l])))
    bias = _bias_table(rel_bias, _band_buckets())
    bias = bias.reshape(3, N_KV_HEADS_A, GROUP_A * BLOCK, 3 * BLOCK)
    return _trunk(x_prompt, x_sample, bias, shared, layers)
```
